```python
import math
import jax, jax.numpy as jnp
from jax import lax
import numpy as np

D_MODEL = 1024
BATCH = 32
SEQ = 2048
DEPTH = 2
DEC_BATCH = 128
DEC_SEQ = 4
PAST_LEN = 16384
PAGE_SIZE = 128

MLA_HEADS = 8
MLA_Q_LORA = 512
MLA_KV_LORA = 256
MLA_NOPE = 128
MLA_ROPE = 64
MLA_V = 128
ROPE_THETA = 10000.0
DIFF_HEADS = 8
DIFF_KV_HEADS = 2
DIFF_GROUP = DIFF_HEADS // DIFF_KV_HEADS
DIFF_HD = 64
REL_BUCKETS = 32
REL_MAX_DIST = 128
D_FF = 4 * D_MODEL
Q_BLOCK = 128
NORM_EPS = 1e-6
SUBLN_EPS = 1e-5
N_MLA_LAYERS = (DEPTH + 1) // 2
N_DIFF_LAYERS = DEPTH // 2

kernel_name = "mla_diffattn_hybrid_step"


def rmsnorm(x, g, eps=NORM_EPS):
    xf = x.astype(jnp.float32)
    y = xf * lax.rsqrt(jnp.mean(xf * xf, axis=-1, keepdims=True) + eps)
    return (y * g.astype(jnp.float32)).astype(x.dtype)


def rope(x, pos):
    half = x.shape[-1] // 2
    inv = ROPE_THETA ** (-jnp.arange(half, dtype=jnp.float32) / half)
    ang = pos.astype(jnp.float32)[:, None] * inv[None, :]
    ang = ang.reshape((ang.shape[0],) + (1,) * (x.ndim - 3) + (half,))
    cos = jnp.cos(ang).astype(x.dtype)
    sin = jnp.sin(ang).astype(x.dtype)
    x1, x2 = x[..., :half], x[..., half:]
    return jnp.concatenate([x1 * cos - x2 * sin, x2 * cos + x1 * sin], axis=-1)


def t5_bucket(qpos, kpos):
    n = jnp.maximum(qpos[:, None] - kpos[None, :], 0)
    max_exact = REL_BUCKETS // 2
    nf = jnp.maximum(n, 1).astype(jnp.float32)
    large = max_exact + (jnp.log(nf / max_exact) / math.log(REL_MAX_DIST / max_exact)
                         * (REL_BUCKETS - max_exact)).astype(jnp.int32)
    large = jnp.minimum(large, REL_BUCKETS - 1)
    return jnp.where(n < max_exact, n, large)


def over_query_blocks(core, q_args, qpos):
    T = qpos.shape[0]
    if T <= Q_BLOCK or T % Q_BLOCK:
        return core(q_args, qpos)
    nb = T // Q_BLOCK

    def split(a):
        return jnp.moveaxis(a.reshape((a.shape[0], nb, Q_BLOCK) + a.shape[2:]), 1, 0)

    out = lax.map(lambda xs: core(xs[0], xs[1]),
                  (tuple(split(a) for a in q_args), qpos.reshape(nb, Q_BLOCK)))
    out = jnp.moveaxis(out, 0, 1)
    return out.reshape((out.shape[0], T) + out.shape[3:])


def mla_mixer(h, pos, past_c, past_kr, w_down, g_q, g_kv, w_uq, w_uk, w_uv, w_o):
    B, T, _ = h.shape
    d = h @ w_down
    cq = rmsnorm(d[..., :MLA_Q_LORA], g_q)
    c = rmsnorm(d[..., MLA_Q_LORA:MLA_Q_LORA + MLA_KV_LORA], g_kv)
    kr = rope(d[..., MLA_Q_LORA + MLA_KV_LORA:], pos)
    q = (cq @ w_uq).reshape(B, T, MLA_HEADS, MLA_NOPE + MLA_ROPE)
    q_rope = rope(q[..., MLA_NOPE:], pos)
    q_lat = jnp.einsum('bthn,chn->bthc', q[..., :MLA_NOPE], w_uk)
    if past_c is None:
        c_all, kr_all, kpos = c, kr, pos
    else:
        c_all = jnp.concatenate([past_c, c], axis=1)
        kr_all = jnp.concatenate([past_kr, kr], axis=1)
        kpos = jnp.arange(past_c.shape[1] + T, dtype=jnp.int32)
    scale = (MLA_NOPE + MLA_ROPE) ** -0.5

    def core(qa, qp):
        ql, qr = qa
        s = (jnp.einsum('bqhc,bkc->bhqk', ql, c_all, preferred_element_type=jnp.float32)
             + jnp.einsum('bqhr,bkr->bhqk', qr, kr_all, preferred_element_type=jnp.float32))
        s = jnp.where(kpos[None, :] <= qp[:, None], s * scale, -jnp.inf)
        p = jax.nn.softmax(s, axis=-1).astype(c_all.dtype)
        return jnp.einsum('bhqk,bkc->bqhc', p, c_all)

    o_lat = over_query_blocks(core, (q_lat, q_rope), pos)
    o = jnp.einsum('bthc,chv->bthv', o_lat, w_uv).reshape(B, T, MLA_HEADS * MLA_V)
    return o @ w_o, c, kr


def diff_mixer(h, pos, past_k, past_v, w_qkv, lam_q1, lam_k1, lam_q2, lam_k2, g_sub, w_o,
               rel_bias, lambda_init):
    B, T, _ = h.shape
    nq = DIFF_HEADS * 2 * DIFF_HD
    nk = DIFF_KV_HEADS * 2 * DIFF_HD
    proj = h @ w_qkv
    q = proj[..., :nq].reshape(B, T, DIFF_KV_HEADS, DIFF_GROUP, 2 * DIFF_HD)
    k = proj[..., nq:nq + nk].reshape(B, T, DIFF_KV_HEADS, 2 * DIFF_HD)
    v = proj[..., nq + nk:].reshape(B, T, DIFF_KV_HEADS, 2 * DIFF_HD)
    f32 = jnp.float32
    lam = (jnp.exp(jnp.sum(lam_q1.astype(f32) * lam_k1.astype(f32)))
           - jnp.exp(jnp.sum(lam_q2.astype(f32) * lam_k2.astype(f32))) + lambda_init)
    if past_k is None:
        k_all, v_all, kpos = k, v, pos
    else:
        k_all = jnp.concatenate([past_k, k], axis=1)
        v_all = jnp.concatenate([past_v, v], axis=1)
        kpos = jnp.arange(past_k.shape[1] + T, dtype=jnp.int32)
    k1, k2 = k_all[..., :DIFF_HD], k_all[..., DIFF_HD:]
    scale = DIFF_HD ** -0.5
    table = rel_bias.astype(f32)

    def core(qa, qp):
        (qb,) = qa
        tq, tk = qp.shape[0], kpos.shape[0]
        mask = kpos[None, :] <= qp[:, None]
        bias = jnp.moveaxis(table[t5_bucket(qp, kpos)], -1, 0).reshape(DIFF_KV_HEADS, DIFF_GROUP, tq, tk)

        def smap(qs, ks):
            s = jnp.einsum('bqngd,bknd->bngqk', qs, ks, preferred_element_type=f32) * scale + bias
            return jax.nn.softmax(jnp.where(mask, s, -jnp.inf), axis=-1)

        p = smap(qb[..., :DIFF_HD], k1) - lam * smap(qb[..., DIFF_HD:], k2)
        return jnp.einsum('bngqk,bknv->bqngv', p.astype(v_all.dtype), v_all)

    o = over_query_blocks(core, (q,), pos)
    o = rmsnorm(o, g_sub, SUBLN_EPS) * (1.0 - lambda_init)
    return o.reshape(B, T, DIFF_HEADS * 2 * DIFF_HD) @ w_o, k, v


def sqrelu_ffn(h, w_up, w_down):
    return jnp.square(jax.nn.relu(h @ w_up)) @ w_down


def setup_inputs(seed: int = 0) -> dict:
    key = jax.random.key(seed)
    ks = iter(jax.random.split(key, 40))
    f32 = jnp.float32
    n_pages = PAST_LEN // PAGE_SIZE
    n_used = DEC_BATCH * n_pages
    n_pool = n_used + n_used // 4
    na, nb = N_MLA_LAYERS, N_DIFF_LAYERS
    nq = DIFF_HEADS * 2 * DIFF_HD
    nk = DIFF_KV_HEADS * 2 * DIFF_HD

    def normal(shape, scale=1.0):
        return jax.random.normal(next(ks), shape, f32) * scale

    def gain(shape):
        return 1.0 + 0.01 * normal(shape)

    x_prompt = normal((BATCH, SEQ, D_MODEL))
    x_sample = normal((DEC_BATCH, DEC_SEQ, D_MODEL))
    cache_mla_ckv = normal((na, n_pool, PAGE_SIZE, MLA_KV_LORA))
    cache_mla_krope = normal((na, n_pool, PAGE_SIZE, MLA_ROPE))
    cache_diff_k = normal((nb, n_pool, PAGE_SIZE, DIFF_KV_HEADS, 2 * DIFF_HD))
    cache_diff_v = normal((nb, n_pool, PAGE_SIZE, DIFF_KV_HEADS, 2 * DIFF_HD))
    page_table = jax.random.permutation(next(ks), n_pool)[:n_used].reshape(DEC_BATCH, n_pages).astype(jnp.int32)
    return {
        'x_prompt': x_prompt,
        'x_sample': x_sample,
        'cache_mla_ckv': cache_mla_ckv,
        'cache_mla_krope': cache_mla_krope,
        'cache_diff_k': cache_diff_k,
        'cache_diff_v': cache_diff_v,
        'page_table': page_table,
        'g_mix': gain((DEPTH, D_MODEL)),
        'g_ffn': gain((DEPTH, D_MODEL)),
        'g_final': gain((D_MODEL,)),
        'w_mla_down': normal((na, D_MODEL, MLA_Q_LORA + MLA_KV_LORA + MLA_ROPE), D_MODEL ** -0.5),
        'g_mla_q': gain((na, MLA_Q_LORA)),
        'g_mla_kv': gain((na, MLA_KV_LORA)),
        'w_mla_uq': normal((na, MLA_Q_LORA, MLA_HEADS * (MLA_NOPE + MLA_ROPE)), MLA_Q_LORA ** -0.5),
        'w_mla_uk': normal((na, MLA_KV_LORA, MLA_HEADS, MLA_NOPE), MLA_KV_LORA ** -0.5),
        'w_mla_uv': normal((na, MLA_KV_LORA, MLA_HEADS, MLA_V), MLA_KV_LORA ** -0.5),
        'w_mla_o': normal((na, MLA_HEADS * MLA_V, D_MODEL), (MLA_HEADS * MLA_V) ** -0.5),
        'w_diff_qkv': normal((nb, D_MODEL, nq + 2 * nk), D_MODEL ** -0.5),
        'lam_q1': normal((nb, DIFF_HD), 0.1),
        'lam_k1': normal((nb, DIFF_HD), 0.1),
        'lam_q2': normal((nb, DIFF_HD), 0.1),
        'lam_k2': normal((nb, DIFF_HD), 0.1),
        'g_diff_sub': gain((nb, 2 * DIFF_HD)),
        'w_diff_o': normal((nb, DIFF_HEADS * 2 * DIFF_HD, D_MODEL), (DIFF_HEADS * 2 * DIFF_HD) ** -0.5),
        'rel_bias': normal((REL_BUCKETS, DIFF_HEADS), 0.5),
        'w_ff_up': normal((DEPTH, D_MODEL, D_FF), D_MODEL ** -0.5),
        'w_ff_down': normal((DEPTH, D_FF, D_MODEL), D_FF ** -0.5),
    }


def reference(x_prompt, x_sample, cache_mla_ckv, cache_mla_krope, cache_diff_k, cache_diff_v,
              page_table, g_mix, g_ffn, g_final, w_mla_down, g_mla_q, g_mla_kv, w_mla_uq,
              w_mla_uk, w_mla_uv, w_mla_o, w_diff_qkv, lam_q1, lam_k1, lam_q2, lam_k2,
              g_diff_sub, w_diff_o, rel_bias, w_ff_up, w_ff_down):
    db, n_pages = page_table.shape
    past_len = n_pages * cache_mla_ckv.shape[2]
    pos_p = jnp.arange(x_prompt.shape[1], dtype=jnp.int32)
    pos_s = past_len + jnp.arange(x_sample.shape[1], dtype=jnp.int32)

    def gather(cache, j):
        g = cache[j, page_table]
        return g.reshape((db, past_len) + g.shape[3:])

    hp, hs = x_prompt, x_sample
    ckv_p, kr_p, k_p, v_p = [], [], [], []
    ckv_s, kr_s, k_s, v_s = [], [], [], []
    for i in range(DEPTH):
        j = i // 2
        a_p = rmsnorm(hp, g_mix[i])
        a_s = rmsnorm(hs, g_mix[i])
        if i % 2 == 0:
            wts = (w_mla_down[j], g_mla_q[j], g_mla_kv[j], w_mla_uq[j], w_mla_uk[j], w_mla_uv[j], w_mla_o[j])
            m_p, c1, r1 = mla_mixer(a_p, pos_p, None, None, *wts)
            m_s, c2, r2 = mla_mixer(a_s, pos_s, gather(cache_mla_ckv, j), gather(cache_mla_krope, j), *wts)
            ckv_p.append(c1); kr_p.append(r1); ckv_s.append(c2); kr_s.append(r2)
        else:
            lambda_init = 0.8 - 0.6 * math.exp(-0.3 * i)
            wts = (w_diff_qkv[j], lam_q1[j], lam_k1[j], lam_q2[j], lam_k2[j], g_diff_sub[j], w_diff_o[j], rel_bias)
            m_p, k1, v1 = diff_mixer(a_p, pos_p, None, None, *wts, lambda_init)
            m_s, k2, v2 = diff_mixer(a_s, pos_s, gather(cache_diff_k, j), gather(cache_diff_v, j), *wts, lambda_init)
            k_p.append(k1); v_p.append(v1); k_s.append(k2); v_s.append(v2)
        hp = hp + m_p
        hs = hs + m_s
        hp = hp + sqrelu_ffn(rmsnorm(hp, g_ffn[i]), w_ff_up[i], w_ff_down[i])
        hs = hs + sqrelu_ffn(rmsnorm(hs, g_ffn[i]), w_ff_up[i], w_ff_down[i])
    y_prompt = rmsnorm(hp, g_final)
    y_sample = rmsnorm(hs, g_final)
    return (y_prompt, y_sample,
            jnp.stack(ckv_p), jnp.stack(kr_p), jnp.stack(k_p), jnp.stack(v_p),
            jnp.stack(ckv_s), jnp.stack(kr_s), jnp.stack(k_s), jnp.stack(v_s))
```

```python
import functools
import math

import numpy as np
import jax
import jax.numpy as jnp
from jax import lax
from jax.experimental import pallas as pl
from jax.experimental.pallas import tpu as pltpu

F32 = jnp.float32
BF16 = jnp.bfloat16

ROPE_THETA = 10000.0
REL_BUCKETS = 32
REL_MAX_DIST = 128
NORM_EPS = 1e-6
SUBLN_EPS = 1e-5
NEG = -1e30

LANES = 128
ROW_TILE = 512
ATTN_TILE = 256
PAGES_PER_STEP = 16
VMEM_LIMIT = 56 * 1024 * 1024


def _cparams(*sem):
    return pltpu.CompilerParams(dimension_semantics=sem, vmem_limit_bytes=VMEM_LIMIT)


def _rms(x, g, eps):
    return x * lax.rsqrt(jnp.mean(x * x, axis=-1, keepdims=True) + eps) * g


def _dot(a, b):
    return jnp.dot(a, b, preferred_element_type=F32)


def _dot_nt(a, b):
    return lax.dot_general(a, b, (((1,), (1,)), ((), ())), preferred_element_type=F32)


def _lane_block_sum(p):
    out = p[:, :LANES]
    for j in range(1, p.shape[1] // LANES):
        out = out + p[:, j * LANES:(j + 1) * LANES]
    return out


def _softmax_step(s, v, m_sc, l_sc, acc_sc):
    m_prev = m_sc[...]
    m_new = jnp.maximum(m_prev, jnp.max(s, axis=-1, keepdims=True))
    alpha = jnp.exp(m_prev - m_new)
    p = jnp.exp(s - jnp.tile(m_new, (1, s.shape[1] // LANES)))
    l_sc[...] = alpha * l_sc[...] + _lane_block_sum(p)
    acc_sc[...] = jnp.tile(alpha, (1, acc_sc.shape[1] // LANES)) * acc_sc[...] + _dot(p.astype(BF16), v)
    m_sc[...] = m_new


def _softmax_init(m_sc, l_sc, acc_sc):
    m_sc[...] = jnp.full(m_sc.shape, NEG, F32)
    l_sc[...] = jnp.zeros(l_sc.shape, F32)
    acc_sc[...] = jnp.zeros(acc_sc.shape, F32)


def _mla_pre_body(h_ref, g_ref, wd_ref, gq_ref, gkv_ref, wuq_ref, wuk_ref, cos_ref, sin_ref,
                  c_ref, kr_ref, kcat_ref, q_ref, *, scale, heads, q_lora, kv_lora, nope, rope):
    tm = h_ref.shape[0]
    a = _rms(h_ref[...], g_ref[...], NORM_EPS).astype(BF16)
    d = _dot(a, wd_ref[...])
    cq = _rms(d[:, :q_lora], gq_ref[...], NORM_EPS)
    c = _rms(d[:, q_lora:q_lora + kv_lora], gkv_ref[...], NORM_EPS)
    cosq = cos_ref[...]
    sinq = sin_ref[...]
    r0 = q_lora + kv_lora
    kr = d[:, r0:r0 + rope] * cosq[:, :rope] + d[:, r0 + rope:r0 + 2 * rope] * sinq[:, :rope]
    c_ref[...] = c
    pad = jnp.zeros((tm, LANES - rope), F32)
    kr_ref[...] = jnp.concatenate([kr, pad], axis=-1).T[:rope]
    kcat_ref[...] = jnp.concatenate([c, kr, pad], axis=-1).astype(BF16)
    q = _dot(cq.astype(BF16), wuq_ref[...])
    hn = heads * nope
    hr = heads * rope
    qr = (q[:, hn:hn + hr] * cosq + q[:, hn + hr:hn + 2 * hr] * sinq) * scale
    for h in range(heads):
        ql = _dot(q[:, h * nope:(h + 1) * nope].astype(BF16), wuk_ref[h]) * scale
        q_ref[h] = jnp.concatenate([ql, qr[:, h * rope:(h + 1) * rope], pad], axis=-1).astype(BF16)


def _mla_pre(h, g, wd, gq, gkv, wuq, wuk, cosq, sinq, *, nb, tm, dims):
    n, dm = h.shape
    t = cosq.shape[0]
    nt = t // tm
    heads, q_lora, kv_lora, nope, rope = dims
    kw = kv_lora + LANES
    row = lambda ti, b: (b * nt + ti, 0)
    const2 = lambda ti, b: (0, 0)
    body = functools.partial(_mla_pre_body, scale=(nope + rope) ** -0.5, heads=heads, q_lora=q_lora,
                             kv_lora=kv_lora, nope=nope, rope=rope)
    return pl.pallas_call(
        body,
        grid=(nt, nb),
        in_specs=[
            pl.BlockSpec((tm, dm), row),
            pl.BlockSpec(g.shape, const2),
            pl.BlockSpec(wd.shape, const2),
            pl.BlockSpec(gq.shape, const2),
            pl.BlockSpec(gkv.shape, const2),
            pl.BlockSpec(wuq.shape, const2),
            pl.BlockSpec(wuk.shape, lambda ti, b: (0, 0, 0)),
            pl.BlockSpec((tm, cosq.shape[1]), lambda ti, b: (ti, 0)),
            pl.BlockSpec((tm, sinq.shape[1]), lambda ti, b: (ti, 0)),
        ],
        out_specs=[
            pl.BlockSpec((tm, kv_lora), row),
            pl.BlockSpec((None, rope, tm), lambda ti, b: (b, 0, ti)),
            pl.BlockSpec((tm, kw), row),
            pl.BlockSpec((heads, tm, kw), lambda ti, b: (0, b * nt + ti, 0)),
        ],
        out_shape=[
            jax.ShapeDtypeStruct((n, kv_lora), F32),
            jax.ShapeDtypeStruct((nb, rope, t), F32),
            jax.ShapeDtypeStruct((n, kw), BF16),
            jax.ShapeDtypeStruct((heads, n, kw), BF16),
        ],
        compiler_params=_cparams("arbitrary", "arbitrary"),
        name="mla_pre",
    )(h, g, wd, gq, gkv, wuq, wuk, cosq, sinq)


def _mla_out(o_heads, wuv_ref, wo_ref, resid):
    ov = [_dot(o.astype(BF16), wuv_ref[h]) for h, o in enumerate(o_heads)]
    o = jnp.concatenate(ov, axis=-1).astype(BF16)
    return resid + _dot(o, wo_ref[...])


def _mla_attn_body(q_ref, k_ref, h_ref, wuv_ref, wo_ref, o_ref, m_sc, l_sc, acc_sc, *, heads, kv_lora, tq):
    i = pl.program_id(1)
    q = q_ref[...].reshape(heads * tq, q_ref.shape[2])
    _softmax_init(m_sc, l_sc, acc_sc)

    def tile(j):
        k = k_ref[pl.ds(pl.multiple_of(j * tq, tq), tq), :]
        return _dot_nt(q, k), k[:, :kv_lora]

    def full_step(j, carry):
        s, v = tile(j)
        _softmax_step(s, v, m_sc, l_sc, acc_sc)
        return carry

    lax.fori_loop(0, i, full_step, 0)
    s, v = tile(i)
    row = lax.broadcasted_iota(jnp.int32, (tq, tq), 0)
    col = lax.broadcasted_iota(jnp.int32, (tq, tq), 1)
    s = jnp.where((col <= row)[None], s.reshape(heads, tq, tq), NEG).reshape(heads * tq, tq)
    _softmax_step(s, v, m_sc, l_sc, acc_sc)

    o = acc_sc[...] / jnp.sum(l_sc[...], axis=-1, keepdims=True)
    o_heads = [o[h * tq:(h + 1) * tq] for h in range(heads)]
    o_ref[...] = _mla_out(o_heads, wuv_ref, wo_ref, h_ref[...])


def _mla_attn_prompt(q, kcat, h, wuv, wo, *, nb, t, tq, kv_lora):
    heads, n, kw = q.shape
    dm = h.shape[1]
    nq = t // tq
    body = functools.partial(_mla_attn_body, heads=heads, kv_lora=kv_lora, tq=tq)
    return pl.pallas_call(
        body,
        grid=(nb, nq),
        in_specs=[
            pl.BlockSpec((heads, tq, kw), lambda b, i: (0, b * nq + i, 0)),
            pl.BlockSpec((t, kw), lambda b, i: (b, 0)),
            pl.BlockSpec((tq, dm), lambda b, i: (b * nq + i, 0)),
            pl.BlockSpec(wuv.shape, lambda b, i: (0, 0, 0)),
            pl.BlockSpec(wo.shape, lambda b, i: (0, 0)),
        ],
        out_specs=pl.BlockSpec((tq, dm), lambda b, i: (b * nq + i, 0)),
        out_shape=jax.ShapeDtypeStruct((n, dm), F32),
        scratch_shapes=[
            pltpu.VMEM((heads * tq, LANES), F32),
            pltpu.VMEM((heads * tq, LANES), F32),
            pltpu.VMEM((heads * tq, kv_lora), F32),
        ],
        compiler_params=_cparams("arbitrary", "arbitrary"),
        name="mla_attn_prompt",
    )(q, kcat, h, wuv, wo)


def _mla_post_body(o_ref, h_ref, wuv_ref, wo_ref, out_ref, *, heads, kv_lora):
    o = o_ref[...]
    o_heads = [o[:, h * kv_lora:(h + 1) * kv_lora] for h in range(heads)]
    out_ref[...] = _mla_out(o_heads, wuv_ref, wo_ref, h_ref[...])


def _mla_post(o_lat, h, wuv, wo, *, tm):
    n, dm = h.shape
    heads, kv_lora, _ = wuv.shape
    body = functools.partial(_mla_post_body, heads=heads, kv_lora=kv_lora)
    return pl.pallas_call(
        body,
        grid=(n // tm,),
        in_specs=[
            pl.BlockSpec((tm, o_lat.shape[1]), lambda i: (i, 0)),
            pl.BlockSpec((tm, dm), lambda i: (i, 0)),
            pl.BlockSpec(wuv.shape, lambda i: (0, 0, 0)),
            pl.BlockSpec(wo.shape, lambda i: (0, 0)),
        ],
        out_specs=pl.BlockSpec((tm, dm), lambda i: (i, 0)),
        out_shape=jax.ShapeDtypeStruct((n, dm), F32),
        compiler_params=_cparams("arbitrary"),
        name="mla_post",
    )(o_lat, h, wuv, wo)


def _ffn_body(h_ref, g_ref, wup_ref, wdn_ref, gf_ref, o_ref, *, chunk, final_norm):
    x = h_ref[...]
    a = _rms(x, g_ref[...], NORM_EPS).astype(BF16)
    acc = x
    for j in range(wup_ref.shape[1] // chunk):
        u = jnp.maximum(_dot(a, wup_ref[:, j * chunk:(j + 1) * chunk]), 0.0)
        acc = acc + _dot((u * u).astype(BF16), wdn_ref[j * chunk:(j + 1) * chunk, :])
    if final_norm:
        acc = _rms(acc, gf_ref[...], NORM_EPS)
    o_ref[...] = acc


def _ffn(h, g, wup, wdn, gf, *, tm, final_norm):
    n, dm = h.shape
    body = functools.partial(_ffn_body, chunk=min(1024, wup.shape[1]), final_norm=final_norm)
    const = lambda i: (0, 0)
    return pl.pallas_call(
        body,
        grid=(n // tm,),
        in_specs=[
            pl.BlockSpec((tm, dm), lambda i: (i, 0)),
            pl.BlockSpec(g.shape, const),
            pl.BlockSpec(wup.shape, const, pipeline_mode=pl.Buffered(1)),
            pl.BlockSpec(wdn.shape, const, pipeline_mode=pl.Buffered(1)),
            pl.BlockSpec(gf.shape, const),
        ],
        out_specs=pl.BlockSpec((tm, dm), lambda i: (i, 0)),
        out_shape=jax.ShapeDtypeStruct((n, dm), F32),
        compiler_params=_cparams("arbitrary"),
        name="ffn_final" if final_norm else "ffn",
    )(h, g, wup, wdn, gf)


def _diff_pre_body(h_ref, g_ref, w_ref, q_ref, k_ref, v_ref, kb_ref, vb_ref, *, scale, heads, hd2, nk):
    a = _rms(h_ref[...], g_ref[...], NORM_EPS).astype(BF16)
    proj = _dot(a, w_ref[...])
    nq = heads * hd2
    for h in range(heads):
        q_ref[h] = (proj[:, h * hd2:(h + 1) * hd2] * scale).astype(BF16)
    k = proj[:, nq:nq + nk]
    v = proj[:, nq + nk:nq + 2 * nk]
    for n in range(nk // hd2):
        k_ref[:, n, :] = k[:, n * hd2:(n + 1) * hd2]
        v_ref[:, n, :] = v[:, n * hd2:(n + 1) * hd2]
    kb_ref[...] = k.astype(BF16)
    vb_ref[...] = v.astype(BF16)


def _diff_pre(h, g, w, *, tm, heads, hd2, nk):
    n, dm = h.shape
    body = functools.partial(_diff_pre_body, scale=(hd2 // 2) ** -0.5, heads=heads, hd2=hd2, nk=nk)
    row = lambda i: (i, 0)
    return pl.pallas_call(
        body,
        grid=(n // tm,),
        in_specs=[
            pl.BlockSpec((tm, dm), row),
            pl.BlockSpec(g.shape, lambda i: (0, 0)),
            pl.BlockSpec(w.shape, lambda i: (0, 0)),
        ],
        out_specs=[
            pl.BlockSpec((heads, tm, hd2), lambda i: (0, i, 0)),
            pl.BlockSpec((tm, nk // hd2, hd2), lambda i: (i, 0, 0)),
            pl.BlockSpec((tm, nk // hd2, hd2), lambda i: (i, 0, 0)),
            pl.BlockSpec((tm, nk), row),
            pl.BlockSpec((tm, nk), row),
        ],
        out_shape=[
            jax.ShapeDtypeStruct((heads, n, hd2), BF16),
            jax.ShapeDtypeStruct((n, nk // hd2, hd2), F32),
            jax.ShapeDtypeStruct((n, nk // hd2, hd2), F32),
            jax.ShapeDtypeStruct((n, nk), BF16),
            jax.ShapeDtypeStruct((n, nk), BF16),
        ],
        compiler_params=_cparams("arbitrary"),
        name="diff_pre",
    )(h, g, w)


def _lambda_value(lam_ref, lambda_init):
    lp = lam_ref[...]
    e1 = jnp.exp(jnp.sum(lp[0:1] * lp[1:2], axis=-1, keepdims=True))
    e2 = jnp.exp(jnp.sum(lp[2:3] * lp[3:4], axis=-1, keepdims=True))
    return e1 - e2 + lambda_init


def _diff_attn_body(q_ref, k_ref, v_ref, d0_ref, d1_ref, h_ref, lam_ref, gsub_ref, wo_ref, o_ref,
                    m_sc, l_sc, acc_sc, *, kv_heads, group, hd2, tq, lambda_init):
    i = pl.program_id(1)
    lam = _lambda_value(lam_ref, lambda_init)
    rows = group * tq
    first_half = lax.broadcasted_iota(jnp.int32, (rows, hd2), 1) < hd2 // 2
    row = lax.broadcasted_iota(jnp.int32, (tq, tq), 0)
    col = lax.broadcasted_iota(jnp.int32, (tq, tq), 1)
    causal = (col <= row)[None, None]
    outs = []
    for n in range(kv_heads):
        qn = q_ref[n * group:(n + 1) * group].reshape(rows, hd2)
        zero = jnp.zeros_like(qn)
        qz = jnp.concatenate([jnp.where(first_half, qn, zero), jnp.where(first_half, zero, qn)], axis=0)
        _softmax_init(m_sc, l_sc, acc_sc)

        def scores(j):
            start = pl.multiple_of(j * tq, tq)
            k = k_ref[pl.ds(start, tq), n * hd2:(n + 1) * hd2]
            v = v_ref[pl.ds(start, tq), n * hd2:(n + 1) * hd2]
            return _dot_nt(qz, k), v

        def far_step(j, carry):
            s, v = scores(j)
            _softmax_step(s, v, m_sc, l_sc, acc_sc)
            return carry

        lax.fori_loop(0, jnp.maximum(i - 1, 0), far_step, 0)

        @pl.when(i >= 1)
        def _():
            s, v = scores(i - 1)
            s = s.reshape(2, group, tq, tq) + d1_ref[n * group:(n + 1) * group][None]
            _softmax_step(s.reshape(2 * rows, tq), v, m_sc, l_sc, acc_sc)

        s, v = scores(i)
        s = s.reshape(2, group, tq, tq) + d0_ref[n * group:(n + 1) * group][None]
        s = jnp.where(causal, s, NEG)
        _softmax_step(s.reshape(2 * rows, tq), v, m_sc, l_sc, acc_sc)

        o = acc_sc[...] / jnp.sum(l_sc[...], axis=-1, keepdims=True)
        o = o[:rows] - lam * o[rows:]
        o = _rms(o, gsub_ref[...], SUBLN_EPS) * (1.0 - lambda_init)
        outs.extend(o[g * tq:(g + 1) * tq] for g in range(group))
    attn = jnp.concatenate(outs, axis=-1).astype(BF16)
    o_ref[...] = h_ref[...] + _dot(attn, wo_ref[...])


def _diff_attn_prompt(q, kb, vb, d0, d1, h, lamp, gsub, wo, *, nb, t, tq, kv_heads, lambda_init):
    heads, n, hd2 = q.shape
    dm = h.shape[1]
    nq = t // tq
    group = heads // kv_heads
    nk = kb.shape[1]
    body = functools.partial(_diff_attn_body, kv_heads=kv_heads, group=group, hd2=hd2, tq=tq,
                             lambda_init=lambda_init)
    c2 = lambda b, i: (0, 0)
    c3 = lambda b, i: (0, 0, 0)
    return pl.pallas_call(
        body,
        grid=(nb, nq),
        in_specs=[
            pl.BlockSpec((heads, tq, hd2), lambda b, i: (0, b * nq + i, 0)),
            pl.BlockSpec((t, nk), lambda b, i: (b, 0)),
            pl.BlockSpec((t, nk), lambda b, i: (b, 0)),
            pl.BlockSpec(d0.shape, c3),
            pl.BlockSpec(d1.shape, c3),
            pl.BlockSpec((tq, dm), lambda b, i: (b * nq + i, 0)),
            pl.BlockSpec(lamp.shape, c2),
            pl.BlockSpec(gsub.shape, c2),
            pl.BlockSpec(wo.shape, c2),
        ],
        out_specs=pl.BlockSpec((tq, dm), lambda b, i: (b * nq + i, 0)),
        out_shape=jax.ShapeDtypeStruct((n, dm), F32),
        scratch_shapes=[
            pltpu.VMEM((2 * group * tq, LANES), F32),
            pltpu.VMEM((2 * group * tq, LANES), F32),
            pltpu.VMEM((2 * group * tq, hd2), F32),
        ],
        compiler_params=_cparams("arbitrary", "arbitrary"),
        name="diff_attn_prompt",
    )(q, kb, vb, d0, d1, h, lamp, gsub, wo)


def _proj_resid_body(x_ref, h_ref, w_ref, o_ref):
    o_ref[...] = h_ref[...] + _dot(x_ref[...].astype(BF16), w_ref[...])


def _proj_resid(x, h, w, *, tm):
    n, dm = h.shape
    return pl.pallas_call(
        _proj_resid_body,
        grid=(n // tm,),
        in_specs=[
            pl.BlockSpec((tm, x.shape[1]), lambda i: (i, 0)),
            pl.BlockSpec((tm, dm), lambda i: (i, 0)),
            pl.BlockSpec(w.shape, lambda i: (0, 0)),
        ],
        out_specs=pl.BlockSpec((tm, dm), lambda i: (i, 0)),
        out_shape=jax.ShapeDtypeStruct((n, dm), F32),
        compiler_params=_cparams("arbitrary"),
        name="proj_resid",
    )(x, h, w)


def _new_token_update(qf, k_new, v_new, bias_new, valid, m_sc, l_sc, acc_sc):
    n_new = valid.shape[1]
    lane0 = lax.broadcasted_iota(jnp.int32, (qf.shape[0], LANES), 1) == 0
    s_cols = []
    for t in range(n_new):
        s_t = jnp.sum(qf * k_new[t:t + 1, :], axis=-1, keepdims=True)
        if bias_new is not None:
            s_t = s_t + bias_new[:, t:t + 1]
        s_cols.append(jnp.where(valid[:, t:t + 1], s_t, NEG))
    m_prev = m_sc[...]
    m_new = m_prev
    for s_t in s_cols:
        m_new = jnp.maximum(m_new, s_t)
    alpha = jnp.exp(m_prev - m_new)
    l = alpha * l_sc[...]
    acc = jnp.tile(alpha, (1, acc_sc.shape[1] // LANES)) * acc_sc[...]
    for t, s_t in enumerate(s_cols):
        p_t = jnp.exp(s_t - m_new)
        l = l + jnp.where(lane0, p_t, 0.0)
        acc = acc + jnp.tile(p_t, (1, acc_sc.shape[1] // LANES)) * v_new[t:t + 1, :]
    m_sc[...] = m_new
    l_sc[...] = l
    acc_sc[...] = acc


def _mla_decode_body(pt_ref, q_ref, knew_ref, *rest, pages, page, kv_lora, rope, n_new):
    ckv_refs = rest[:pages]
    kr_refs = rest[pages:2 * pages]
    o_ref, cbuf, krbuf, m_sc, l_sc, acc_sc = rest[2 * pages:]
    c = pl.program_id(1)

    @pl.when(c == 0)
    def _():
        _softmax_init(m_sc, l_sc, acc_sc)

    krbuf[rope:, :] = jnp.zeros((krbuf.shape[0] - rope, krbuf.shape[1]), BF16)
    for p in range(pages):
        cbuf[p * page:(p + 1) * page, :] = ckv_refs[p][...].astype(BF16)
        krbuf[:rope, p * page:(p + 1) * page] = kr_refs[p][...].astype(BF16)
    q = q_ref[...]
    ck = cbuf[...]
    s = _dot_nt(q[:, :kv_lora], ck) + _dot(q[:, kv_lora:], krbuf[...])
    _softmax_step(s, ck, m_sc, l_sc, acc_sc)

    @pl.when(c == pl.num_programs(1) - 1)
    def _():
        r = q.shape[0]
        k_new = knew_ref[...]
        t_row = lax.broadcasted_iota(jnp.int32, (r, n_new), 0) % n_new
        t_col = lax.broadcasted_iota(jnp.int32, (r, n_new), 1)
        _new_token_update(q.astype(F32), k_new, k_new[:, :kv_lora], None, t_col <= t_row, m_sc, l_sc, acc_sc)
        o_ref[...] = acc_sc[...] / jnp.sum(l_sc[...], axis=-1, keepdims=True)


def _mla_decode(page_table, q, knew, cache_ckv, cache_kr, *, pages, n_new):
    db, r, kw = q.shape
    n_pages = page_table.shape[1]
    page, kv_lora = cache_ckv.shape[1:]
    rope = cache_kr.shape[1]
    nchunks = n_pages // pages
    pt = page_table.reshape(-1)

    def page_spec(p, shape):
        return pl.BlockSpec((None,) + shape, lambda b, c, pt_ref: (pt_ref[b * n_pages + c * pages + p], 0, 0))

    body = functools.partial(_mla_decode_body, pages=pages, page=page, kv_lora=kv_lora, rope=rope, n_new=n_new)
    grid_spec = pltpu.PrefetchScalarGridSpec(
        num_scalar_prefetch=1,
        grid=(db, nchunks),
        in_specs=[
            pl.BlockSpec((None, r, kw), lambda b, c, pt_ref: (b, 0, 0)),
            pl.BlockSpec((None,) + knew.shape[1:], lambda b, c, pt_ref: (b, 0, 0)),
        ] + [page_spec(p, (page, kv_lora)) for p in range(pages)]
          + [page_spec(p, (rope, page)) for p in range(pages)],
        out_specs=pl.BlockSpec((None, r, kv_lora), lambda b, c, pt_ref: (b, 0, 0)),
        scratch_shapes=[
            pltpu.VMEM((pages * page, kv_lora), BF16),
            pltpu.VMEM((kw - kv_lora, pages * page), BF16),
            pltpu.VMEM((r, LANES), F32),
            pltpu.VMEM((r, LANES), F32),
            pltpu.VMEM((r, kv_lora), F32),
        ],
    )
    return pl.pallas_call(
        body,
        grid_spec=grid_spec,
        out_shape=jax.ShapeDtypeStruct((db, r, kv_lora), F32),
        compiler_params=_cparams("arbitrary", "arbitrary"),
        name="mla_decode",
    )(pt, q, knew, *([cache_ckv] * pages), *([cache_kr] * pages))


def _diff_decode_body(pt_ref, q_ref, knew_ref, vnew_ref, blast_ref, bnew_ref, lam_ref, gsub_ref, *rest,
                      pages, page, kv_heads, group, hd2, n_new, lambda_init):
    k_refs = rest[:pages]
    v_refs = rest[pages:2 * pages]
    o_ref, kbuf, vbuf, s_sc, m_sc, l_sc, acc_sc = rest[2 * pages:]
    c = pl.program_id(1)
    last = pl.num_programs(1) - 1

    @pl.when(c == 0)
    def _():
        _softmax_init(m_sc, l_sc, acc_sc)

    for p in range(pages):
        rows = slice(p * page, (p + 1) * page)
        for n in range(kv_heads):
            lanes = slice(n * hd2, (n + 1) * hd2)
            kbuf[rows, lanes] = k_refs[p][:, n, :].astype(BF16)
            vbuf[rows, lanes] = v_refs[p][:, n, :].astype(BF16)
    q = q_ref[...]
    s_sc[...] = _dot_nt(q, kbuf[...])

    @pl.when(c == last)
    def _():
        cols = slice((pages - 1) * page, pages * page)
        s_sc[:, cols] = s_sc[:, cols] + blast_ref[...]

    _softmax_step(s_sc[...], vbuf[...], m_sc, l_sc, acc_sc)

    @pl.when(c == last)
    def _():
        r = q.shape[0]
        t_row = lax.broadcasted_iota(jnp.int32, (r, n_new), 0) % n_new
        t_col = lax.broadcasted_iota(jnp.int32, (r, n_new), 1)
        _new_token_update(q.astype(F32), knew_ref[...], vnew_ref[...], bnew_ref[...], t_col <= t_row,
                          m_sc, l_sc, acc_sc)
        o = acc_sc[...] / jnp.sum(l_sc[...], axis=-1, keepdims=True)
        lam = _lambda_value(lam_ref, lambda_init)
        gr = group * n_new
        outs = []
        for n in range(kv_heads):
            o1 = o[(2 * n) * gr:(2 * n + 1) * gr, n * hd2:(n + 1) * hd2]
            o2 = o[(2 * n + 1) * gr:(2 * n + 2) * gr, n * hd2:(n + 1) * hd2]
            outs.append(o1 - lam * o2)
        on = jnp.concatenate(outs, axis=0)
        o_ref[...] = _rms(on, gsub_ref[...], SUBLN_EPS) * (1.0 - lambda_init)


def _diff_decode(page_table, q, knew, vnew, blast, bnew, lamp, gsub, cache_k, cache_v, *, pages, kv_heads,
                 group, n_new, lambda_init):
    db, r, nk = q.shape
    n_pages = page_table.shape[1]
    page = cache_k.shape[1]
    hd2 = nk // kv_heads
    nchunks = n_pages // pages
    pt = page_table.reshape(-1)

    def page_spec(p):
        return pl.BlockSpec((None, page, kv_heads, hd2),
                            lambda b, c, pt_ref: (pt_ref[b * n_pages + c * pages + p], 0, 0, 0))

    per_b = lambda b, c, pt_ref: (b, 0, 0)
    const = lambda b, c, pt_ref: (0, 0)
    body = functools.partial(_diff_decode_body, pages=pages, page=page, kv_heads=kv_heads, group=group,
                             hd2=hd2, n_new=n_new, lambda_init=lambda_init)
    r_out = kv_heads * group * n_new
    grid_spec = pltpu.PrefetchScalarGridSpec(
        num_scalar_prefetch=1,
        grid=(db, nchunks),
        in_specs=[
            pl.BlockSpec((None, r, nk), per_b),
            pl.BlockSpec((None,) + knew.shape[1:], per_b),
            pl.BlockSpec((None,) + vnew.shape[1:], per_b),
            pl.BlockSpec(blast.shape, const),
            pl.BlockSpec(bnew.shape, const),
            pl.BlockSpec(lamp.shape, const),
            pl.BlockSpec(gsub.shape, const),
        ] + [page_spec(p) for p in range(pages)] * 2,
        out_specs=pl.BlockSpec((None, r_out, hd2), per_b),
        scratch_shapes=[
            pltpu.VMEM((pages * page, nk), BF16),
            pltpu.VMEM((pages * page, nk), BF16),
            pltpu.VMEM((r, pages * page), F32),
            pltpu.VMEM((r, LANES), F32),
            pltpu.VMEM((r, LANES), F32),
            pltpu.VMEM((r, nk), F32),
        ],
    )
    return pl.pallas_call(
        body,
        grid_spec=grid_spec,
        out_shape=jax.ShapeDtypeStruct((db, r_out, hd2), F32),
        compiler_params=_cparams("arbitrary", "arbitrary"),
        name="diff_decode",
    )(pt, q, knew, vnew, blast, bnew, lamp, gsub, *([cache_k] * pages), *([cache_v] * pages))


def _rope_tables(pos, half, heads):
    inv = ROPE_THETA ** (-jnp.arange(half, dtype=F32) / half)
    ang = pos.astype(F32)[:, None] * inv[None, :]
    cos, sin = jnp.cos(ang), jnp.sin(ang)
    cos2 = jnp.concatenate([cos, cos], axis=-1)
    sin2 = jnp.concatenate([-sin, sin], axis=-1)
    return jnp.tile(cos2, (1, heads)), jnp.tile(sin2, (1, heads))


def _bucket_of_distance(n):
    max_exact = REL_BUCKETS // 2
    nf = np.maximum(n, 1).astype(np.float32)
    ratio = np.log(nf / np.float32(max_exact)) / np.float32(math.log(REL_MAX_DIST / max_exact))
    large = max_exact + (ratio * np.float32(REL_BUCKETS - max_exact)).astype(np.int32)
    large = np.minimum(large, REL_BUCKETS - 1)
    return np.where(n < max_exact, n, large)


def _shifted_bias(rel_bias, dist, far_from, max_dist):
    far = _bucket_of_distance(np.arange(far_from, max_dist + 1))
    assert (far == far[0]).all(), "relative-position bias must be constant beyond the near tiles"
    table = rel_bias.astype(F32) - rel_bias[int(far[0])].astype(F32)[None, :]
    buckets = _bucket_of_distance(np.maximum(dist, 0))
    return jnp.moveaxis(table[jnp.asarray(buckets)], -1, 0)


def kernel(x_prompt, x_sample, cache_mla_ckv, cache_mla_krope, cache_diff_k, cache_diff_v, page_table, g_mix, g_ffn, g_final, w_mla_down, g_mla_q, g_mla_kv, w_mla_uq, w_mla_uk, w_mla_uv, w_mla_o, w_diff_qkv, lam_q1, lam_k1, lam_q2, lam_k2, g_diff_sub, w_diff_o, rel_bias, w_ff_up, w_ff_down):
    nb, t, dm = x_prompt.shape
    db, n_new, _ = x_sample.shape
    n_pages = page_table.shape[1]
    page = cache_mla_ckv.shape[2]
    past = n_pages * page
    kv_lora, heads, nope = w_mla_uk.shape[1:]
    q_lora = g_mla_q.shape[1]
    rope = cache_mla_krope.shape[3]
    kv_heads, hd2 = cache_diff_k.shape[3:]
    nk = kv_heads * hd2
    dheads = rel_bias.shape[1]
    group = dheads // kv_heads
    assert g_mix.shape[0] == 2 and w_mla_down.shape[0] == 1 and w_diff_qkv.shape[0] == 1
    assert heads * rope == 4 * LANES and kv_lora % LANES == 0 and hd2 == LANES and rope * 2 == LANES

    tm_p = min(ROW_TILE, t)
    ns = db * n_new
    tm_s = min(ROW_TILE, ns)
    tq = min(ATTN_TILE, t)
    pages = min(PAGES_PER_STEP, n_pages)
    assert t % tm_p == 0 and ns % tm_s == 0 and t % tq == 0 and n_pages % pages == 0
    lambda_init = 0.8 - 0.6 * math.exp(-0.3 * 1)

    row = lambda v: v.reshape(1, -1).astype(F32)
    bf = lambda w: w.astype(BF16)

    wd = w_mla_down[0]
    r0 = q_lora + kv_lora
    half = rope // 2
    wd_ext = bf(jnp.concatenate([wd, wd[:, r0 + half:r0 + rope], wd[:, r0:r0 + half]], axis=1))
    wuq = w_mla_uq[0].reshape(q_lora, heads, nope + rope)
    wuq_ext = bf(jnp.concatenate([
        wuq[:, :, :nope].reshape(q_lora, heads * nope),
        wuq[:, :, nope:].reshape(q_lora, heads * rope),
        jnp.concatenate([wuq[:, :, nope + half:], wuq[:, :, nope:nope + half]], axis=-1).reshape(q_lora, heads * rope),
    ], axis=1))
    wuk_t = bf(jnp.transpose(w_mla_uk[0], (1, 2, 0)))
    wuv = bf(jnp.transpose(w_mla_uv[0], (1, 0, 2)))
    wo_mla = bf(w_mla_o[0])
    wqkv = bf(w_diff_qkv[0])
    wo_diff = bf(w_diff_o[0])
    wup = bf(w_ff_up)
    wdn = bf(w_ff_down)
    lamp = jnp.stack([lam_q1[0], lam_k1[0], lam_q2[0], lam_k2[0]]).astype(F32)
    gsub = row(g_diff_sub[0])
    mla_dims = (heads, q_lora, kv_lora, nope, rope)

    cos_p, sin_p = _rope_tables(jnp.arange(t, dtype=jnp.int32), half, heads)
    pos_s = past + (jnp.arange(ns, dtype=jnp.int32) % n_new)
    cos_s, sin_s = _rope_tables(pos_s, half, heads)
    ar = np.arange(tq)
    d0 = _shifted_bias(rel_bias, ar[:, None] - ar[None, :], tq + 1, t + tq)
    d1 = _shifted_bias(rel_bias, tq + ar[:, None] - ar[None, :], tq + 1, t + tq)

    hp = x_prompt.reshape(nb * t, dm)
    hs = x_sample.reshape(ns, dm)

    c_p, krt_p, kcat_p, q_p = _mla_pre(hp, row(g_mix[0]), wd_ext, row(g_mla_q[0]), row(g_mla_kv[0]), wuq_ext,
                                       wuk_t, cos_p, sin_p, nb=nb, tm=tm_p, dims=mla_dims)
    hp = _mla_attn_prompt(q_p, kcat_p, hp, wuv, wo_mla, nb=nb, t=t, tq=tq, kv_lora=kv_lora)
    hp = _ffn(hp, row(g_ffn[0]), wup[0], wdn[0], row(g_final), tm=tm_p, final_norm=False)

    c_s, krt_s, _, q_s = _mla_pre(hs, row(g_mix[0]), wd_ext, row(g_mla_q[0]), row(g_mla_kv[0]), wuq_ext,
                                  wuk_t, cos_s, sin_s, nb=1, tm=tm_s, dims=mla_dims)
    kw = kv_lora + LANES
    q_dec = q_s.reshape(heads, db, n_new, kw).transpose(1, 0, 2, 3).reshape(db, heads * n_new, kw)
    kr_s = krt_s[0].T
    knew = jnp.concatenate([c_s, kr_s, jnp.zeros((ns, LANES - rope), F32)], axis=-1).reshape(db, n_new, kw)
    knew = jnp.pad(knew, ((0, 0), (0, 8 - n_new), (0, 0)))
    o_lat = _mla_decode(page_table, q_dec, knew, cache_mla_ckv[0], jnp.swapaxes(cache_mla_krope[0], 1, 2),
                        pages=pages, n_new=n_new)
    o_lat = o_lat.reshape(db, heads, n_new, kv_lora).transpose(0, 2, 1, 3).reshape(ns, heads * kv_lora)
    hs = _mla_post(o_lat, hs, wuv, wo_mla, tm=tm_s)
    hs = _ffn(hs, row(g_ffn[0]), wup[0], wdn[0], row(g_final), tm=tm_s, final_norm=False)

    qd_p, k_p, v_p, kb_p, vb_p = _diff_pre(hp, row(g_mix[1]), wqkv, tm=tm_p, heads=dheads, hd2=hd2, nk=nk)
    hp = _diff_attn_prompt(qd_p, kb_p, vb_p, d0, d1, hp, lamp, gsub, wo_diff, nb=nb, t=t, tq=tq,
                           kv_heads=kv_heads, lambda_init=lambda_init)
    y_p = _ffn(hp, row(g_ffn[1]), wup[1], wdn[1], row(g_final), tm=tm_p, final_norm=True)

    qd_s, k_s, v_s, _, _ = _diff_pre(hs, row(g_mix[1]), wqkv, tm=tm_s, heads=dheads, hd2=hd2, nk=nk)
    q6 = qd_s.reshape(kv_heads, group, db, n_new, 2, hd2 // 2).transpose(2, 0, 4, 1, 3, 5)
    place = jnp.eye(2 * kv_heads, dtype=BF16).reshape(kv_heads, 2, 2 * kv_heads)
    q_dec = (q6[..., None, :] * place[None, :, :, None, None, :, None]).reshape(db, 2 * dheads * n_new, nk)
    pad_new = lambda a: jnp.pad(a.reshape(db, n_new, nk), ((0, 0), (0, 8 - n_new), (0, 0)))
    r_idx = np.arange(2 * dheads * n_new)
    r_head = (r_idx // (2 * group * n_new)) * group + (r_idx // n_new) % group
    r_tok = r_idx % n_new
    far_from = page + 1
    dist_last = page + r_tok[:, None] - np.arange(page)[None, :]
    dist_new = r_tok[:, None] - np.arange(n_new)[None, :]
    b_last = _shifted_bias(rel_bias, dist_last, far_from, past + n_new)
    b_new = _shifted_bias(rel_bias, dist_new, far_from, past + n_new)
    pick = jnp.asarray(r_head)[None, :, None]
    b_last = jnp.take_along_axis(b_last, pick, axis=0)[0]
    b_new = jnp.pad(jnp.take_along_axis(b_new, pick, axis=0)[0], ((0, 0), (0, LANES - n_new)))
    o_d = _diff_decode(page_table, q_dec, pad_new(k_s), pad_new(v_s), b_last, b_new, lamp, gsub,
                       cache_diff_k[0], cache_diff_v[0],
                       pages=pages, kv_heads=kv_heads, group=group, n_new=n_new, lambda_init=lambda_init)
    o_d = o_d.reshape(db, kv_heads, group, n_new, hd2).transpose(0, 3, 1, 2, 4).reshape(ns, dheads * hd2)
    hs = _proj_resid(o_d, hs, wo_diff, tm=tm_s)
    y_s = _ffn(hs, row(g_ffn[1]), wup[1], wdn[1], row(g_final), tm=tm_s, final_norm=True)

    return (y_p.reshape(nb, t, dm), y_s.reshape(db, n_new, dm),
            c_p.reshape(1, nb, t, kv_lora), jnp.swapaxes(krt_p, 1, 2)[None],
            k_p.reshape(1, nb, t, kv_heads, hd2), v_p.reshape(1, nb, t, kv_heads, hd2),
            c_s.reshape(1, db, n_new, kv_lora), kr_s.reshape(1, db, n_new, rope),
            k_s.reshape(1, db, n_new, kv_heads, hd2), v_s.reshape(1, db, n_new, kv_heads, hd2))
```

```python
import functools
import math

import numpy as np
import jax
import jax.numpy as jnp
from jax import lax
from jax.experimental import pallas as pl
from jax.experimental.pallas import tpu as pltpu

F32 = jnp.float32
BF16 = jnp.bfloat16

ROPE_THETA = 10000.0
REL_BUCKETS = 32
REL_MAX_DIST = 128
NORM_EPS = 1e-6
SUBLN_EPS = 1e-5
NEG = -1e30
LOG2E = math.log2(math.e)

LANES = 128
ROW_TILE = 512
ATTN_TILE = 256
PAGES_PER_STEP = 16
VMEM_LIMIT = 56 * 1024 * 1024


def _cparams(*sem):
    return pltpu.CompilerParams(dimension_semantics=sem, vmem_limit_bytes=VMEM_LIMIT)


def _rms(x, g, eps):
    return x * lax.rsqrt(jnp.mean(x * x, axis=-1, keepdims=True) + eps) * g


def _dot(a, b):
    return jnp.dot(a, b, preferred_element_type=F32)


def _dot_nt(a, b):
    return lax.dot_general(a, b, (((1,), (1,)), ((), ())), preferred_element_type=F32)


def _lane_block_sum(p):
    out = p[:, :LANES]
    for j in range(1, p.shape[1] // LANES):
        out = out + p[:, j * LANES:(j + 1) * LANES]
    return out


def _softmax_step(s, v, m_sc, l_sc, acc_sc):
    m_prev = m_sc[...]
    m_new = jnp.maximum(m_prev, jnp.max(s, axis=-1, keepdims=True))
    alpha = jnp.exp2(m_prev - m_new)
    p = jnp.exp2(s - jnp.tile(m_new, (1, s.shape[1] // LANES)))
    l_sc[...] = alpha * l_sc[...] + _lane_block_sum(p)
    acc_sc[...] = jnp.tile(alpha, (1, acc_sc.shape[1] // LANES)) * acc_sc[...] + _dot(p.astype(BF16), v)
    m_sc[...] = m_new


def _softmax_init(m_sc, l_sc, acc_sc):
    m_sc[...] = jnp.full(m_sc.shape, NEG, F32)
    l_sc[...] = jnp.zeros(l_sc.shape, F32)
    acc_sc[...] = jnp.zeros(acc_sc.shape, F32)


def _mla_pre_body(h_ref, g_ref, wd_ref, gq_ref, gkv_ref, wuq_ref, wuk_ref, cos_ref, sin_ref,
                  c_ref, kr_ref, kcat_ref, q_ref, *, scale, heads, q_lora, kv_lora, nope, rope):
    tm = h_ref.shape[0]
    a = _rms(h_ref[...], g_ref[...], NORM_EPS).astype(BF16)
    d = _dot(a, wd_ref[...])
    cq = _rms(d[:, :q_lora], gq_ref[...], NORM_EPS)
    c = _rms(d[:, q_lora:q_lora + kv_lora], gkv_ref[...], NORM_EPS)
    cosq = cos_ref[...]
    sinq = sin_ref[...]
    r0 = q_lora + kv_lora
    kr = d[:, r0:r0 + rope] * cosq[:, :rope] + d[:, r0 + rope:r0 + 2 * rope] * sinq[:, :rope]
    c_ref[...] = c
    pad = jnp.zeros((tm, LANES - rope), F32)
    kr_ref[...] = jnp.concatenate([kr, pad], axis=-1).T[:rope]
    kcat_ref[...] = jnp.concatenate([c, kr, pad], axis=-1).astype(BF16)
    q = _dot(cq.astype(BF16), wuq_ref[...])
    hn = heads * nope
    hr = heads * rope
    qr = (q[:, hn:hn + hr] * cosq + q[:, hn + hr:hn + 2 * hr] * sinq) * scale
    for h in range(heads):
        ql = _dot(q[:, h * nope:(h + 1) * nope].astype(BF16), wuk_ref[h]) * scale
        q_ref[h] = jnp.concatenate([ql, qr[:, h * rope:(h + 1) * rope], pad], axis=-1).astype(BF16)


def _mla_pre(h, g, wd, gq, gkv, wuq, wuk, cosq, sinq, *, nb, tm, dims):
    n, dm = h.shape
    t = cosq.shape[0]
    nt = t // tm
    heads, q_lora, kv_lora, nope, rope = dims
    kw = kv_lora + LANES
    row = lambda ti, b: (b * nt + ti, 0)
    const2 = lambda ti, b: (0, 0)
    body = functools.partial(_mla_pre_body, scale=LOG2E * (nope + rope) ** -0.5, heads=heads, q_lora=q_lora,
                             kv_lora=kv_lora, nope=nope, rope=rope)
    return pl.pallas_call(
        body,
        grid=(nt, nb),
        in_specs=[
            pl.BlockSpec((tm, dm), row),
            pl.BlockSpec(g.shape, const2),
            pl.BlockSpec(wd.shape, const2),
            pl.BlockSpec(gq.shape, const2),
            pl.BlockSpec(gkv.shape, const2),
            pl.BlockSpec(wuq.shape, const2),
            pl.BlockSpec(wuk.shape, lambda ti, b: (0, 0, 0)),
            pl.BlockSpec((tm, cosq.shape[1]), lambda ti, b: (ti, 0)),
            pl.BlockSpec((tm, sinq.shape[1]), lambda ti, b: (ti, 0)),
        ],
        out_specs=[
            pl.BlockSpec((tm, kv_lora), row),
            pl.BlockSpec((None, rope, tm), lambda ti, b: (b, 0, ti)),
            pl.BlockSpec((tm, kw), row),
            pl.BlockSpec((heads, tm, kw), lambda ti, b: (0, b * nt + ti, 0)),
        ],
        out_shape=[
            jax.ShapeDtypeStruct((n, kv_lora), F32),
            jax.ShapeDtypeStruct((nb, rope, t), F32),
            jax.ShapeDtypeStruct((n, kw), BF16),
            jax.ShapeDtypeStruct((heads, n, kw), BF16),
        ],
        compiler_params=_cparams("arbitrary", "arbitrary"),
        name="mla_pre",
    )(h, g, wd, gq, gkv, wuq, wuk, cosq, sinq)


def _mla_out(o_heads, wuv_ref, wo_ref, resid):
    ov = [_dot(o.astype(BF16), wuv_ref[h]) for h, o in enumerate(o_heads)]
    o = jnp.concatenate(ov, axis=-1).astype(BF16)
    return resid + _dot(o, wo_ref[...])


def _mla_attn_body(q_ref, k_ref, h_ref, wuv_ref, wo_ref, o_ref, m_sc, l_sc, acc_sc, *, heads, kv_lora, tq):
    i = pl.program_id(1)
    q = q_ref[...].reshape(heads * tq, q_ref.shape[2])
    _softmax_init(m_sc, l_sc, acc_sc)

    def keys(j):
        return k_ref[pl.ds(pl.multiple_of(j * tq, tq), tq), :]

    def full_step(j, s):
        s_next = _dot_nt(q, keys(j + 1))
        _softmax_step(s, keys(j)[:, :kv_lora], m_sc, l_sc, acc_sc)
        return s_next

    s = lax.fori_loop(0, i, full_step, _dot_nt(q, keys(0)))
    row = lax.broadcasted_iota(jnp.int32, (tq, tq), 0)
    col = lax.broadcasted_iota(jnp.int32, (tq, tq), 1)
    s = jnp.where((col <= row)[None], s.reshape(heads, tq, tq), NEG).reshape(heads * tq, tq)
    _softmax_step(s, keys(i)[:, :kv_lora], m_sc, l_sc, acc_sc)

    o = acc_sc[...] / jnp.sum(l_sc[...], axis=-1, keepdims=True)
    o_heads = [o[h * tq:(h + 1) * tq] for h in range(heads)]
    o_ref[...] = _mla_out(o_heads, wuv_ref, wo_ref, h_ref[...])


def _mla_attn_prompt(q, kcat, h, wuv, wo, *, nb, t, tq, kv_lora):
    heads, n, kw = q.shape
    dm = h.shape[1]
    nq = t // tq
    body = functools.partial(_mla_attn_body, heads=heads, kv_lora=kv_lora, tq=tq)
    return pl.pallas_call(
        body,
        grid=(nb, nq),
        in_specs=[
            pl.BlockSpec((heads, tq, kw), lambda b, i: (0, b * nq + i, 0)),
            pl.BlockSpec((t, kw), lambda b, i: (b, 0)),
            pl.BlockSpec((tq, dm), lambda b, i: (b * nq + i, 0)),
            pl.BlockSpec(wuv.shape, lambda b, i: (0, 0, 0)),
            pl.BlockSpec(wo.shape, lambda b, i: (0, 0)),
        ],
        out_specs=pl.BlockSpec((tq, dm), lambda b, i: (b * nq + i, 0)),
        out_shape=jax.ShapeDtypeStruct((n, dm), F32),
        scratch_shapes=[
            pltpu.VMEM((heads * tq, LANES), F32),
            pltpu.VMEM((heads * tq, LANES), F32),
            pltpu.VMEM((heads * tq, kv_lora), F32),
        ],
        compiler_params=_cparams("arbitrary", "arbitrary"),
        name="mla_attn_prompt",
    )(q, kcat, h, wuv, wo)


def _mla_post_body(o_ref, h_ref, wuv_ref, wo_ref, out_ref, *, heads, kv_lora):
    o = o_ref[...]
    o_heads = [o[:, h * kv_lora:(h + 1) * kv_lora] for h in range(heads)]
    out_ref[...] = _mla_out(o_heads, wuv_ref, wo_ref, h_ref[...])


def _mla_post(o_lat, h, wuv, wo, *, tm):
    n, dm = h.shape
    heads, kv_lora, _ = wuv.shape
    body = functools.partial(_mla_post_body, heads=heads, kv_lora=kv_lora)
    return pl.pallas_call(
        body,
        grid=(n // tm,),
        in_specs=[
            pl.BlockSpec((tm, o_lat.shape[1]), lambda i: (i, 0)),
            pl.BlockSpec((tm, dm), lambda i: (i, 0)),
            pl.BlockSpec(wuv.shape, lambda i: (0, 0, 0)),
            pl.BlockSpec(wo.shape, lambda i: (0, 0)),
        ],
        out_specs=pl.BlockSpec((tm, dm), lambda i: (i, 0)),
        out_shape=jax.ShapeDtypeStruct((n, dm), F32),
        compiler_params=_cparams("arbitrary"),
        name="mla_post",
    )(o_lat, h, wuv, wo)


def _ffn_body(h_ref, g_ref, wup_ref, wdn_ref, gf_ref, o_ref, *, chunk, final_norm):
    x = h_ref[...]
    a = _rms(x, g_ref[...], NORM_EPS).astype(BF16)
    acc = x
    for j in range(wup_ref.shape[1] // chunk):
        u = jnp.maximum(_dot(a, wup_ref[:, j * chunk:(j + 1) * chunk]), 0.0)
        acc = acc + _dot((u * u).astype(BF16), wdn_ref[j * chunk:(j + 1) * chunk, :])
    if final_norm:
        acc = _rms(acc, gf_ref[...], NORM_EPS)
    o_ref[...] = acc


def _ffn(h, g, wup, wdn, gf, *, tm, final_norm):
    n, dm = h.shape
    body = functools.partial(_ffn_body, chunk=min(1024, wup.shape[1]), final_norm=final_norm)
    const = lambda i: (0, 0)
    return pl.pallas_call(
        body,
        grid=(n // tm,),
        in_specs=[
            pl.BlockSpec((tm, dm), lambda i: (i, 0)),
            pl.BlockSpec(g.shape, const),
            pl.BlockSpec(wup.shape, const, pipeline_mode=pl.Buffered(1)),
            pl.BlockSpec(wdn.shape, const, pipeline_mode=pl.Buffered(1)),
            pl.BlockSpec(gf.shape, const),
        ],
        out_specs=pl.BlockSpec((tm, dm), lambda i: (i, 0)),
        out_shape=jax.ShapeDtypeStruct((n, dm), F32),
        compiler_params=_cparams("arbitrary"),
        name="ffn_final" if final_norm else "ffn",
    )(h, g, wup, wdn, gf)


def _diff_pre_body(h_ref, g_ref, w_ref, q_ref, k_ref, v_ref, kb_ref, vb_ref, *, scale, heads, hd2, nk):
    a = _rms(h_ref[...], g_ref[...], NORM_EPS).astype(BF16)
    proj = _dot(a, w_ref[...])
    nq = heads * hd2
    for h in range(heads):
        q_ref[h] = (proj[:, h * hd2:(h + 1) * hd2] * scale).astype(BF16)
    k = proj[:, nq:nq + nk]
    v = proj[:, nq + nk:nq + 2 * nk]
    kv_heads = nk // hd2
    tm = h_ref.shape[0]
    for n in range(kv_heads):
        k_ref[pl.ds(n, tm, stride=kv_heads), :] = k[:, n * hd2:(n + 1) * hd2]
        v_ref[pl.ds(n, tm, stride=kv_heads), :] = v[:, n * hd2:(n + 1) * hd2]
    kb_ref[...] = k.astype(BF16)
    vb_ref[...] = v.astype(BF16)


def _diff_pre(h, g, w, *, tm, heads, hd2, nk):
    n, dm = h.shape
    body = functools.partial(_diff_pre_body, scale=LOG2E * (hd2 // 2) ** -0.5, heads=heads, hd2=hd2, nk=nk)
    row = lambda i: (i, 0)
    return pl.pallas_call(
        body,
        grid=(n // tm,),
        in_specs=[
            pl.BlockSpec((tm, dm), row),
            pl.BlockSpec(g.shape, lambda i: (0, 0)),
            pl.BlockSpec(w.shape, lambda i: (0, 0)),
        ],
        out_specs=[
            pl.BlockSpec((heads, tm, hd2), lambda i: (0, i, 0)),
            pl.BlockSpec((tm * nk // hd2, hd2), row),
            pl.BlockSpec((tm * nk // hd2, hd2), row),
            pl.BlockSpec((tm, nk), row),
            pl.BlockSpec((tm, nk), row),
        ],
        out_shape=[
            jax.ShapeDtypeStruct((heads, n, hd2), BF16),
            jax.ShapeDtypeStruct((n * nk // hd2, hd2), F32),
            jax.ShapeDtypeStruct((n * nk // hd2, hd2), F32),
            jax.ShapeDtypeStruct((n, nk), BF16),
            jax.ShapeDtypeStruct((n, nk), BF16),
        ],
        compiler_params=_cparams("arbitrary"),
        name="diff_pre",
    )(h, g, w)


def _lambda_value(lam_ref, lambda_init):
    lp = lam_ref[...]
    e1 = jnp.exp(jnp.sum(lp[0:1] * lp[1:2], axis=-1, keepdims=True))
    e2 = jnp.exp(jnp.sum(lp[2:3] * lp[3:4], axis=-1, keepdims=True))
    return e1 - e2 + lambda_init


def _diff_attn_body(q_ref, k_ref, v_ref, d0_ref, d1_ref, h_ref, lam_ref, gsub_ref, wo_ref, o_ref,
                    m_sc, l_sc, acc_sc, *, kv_heads, group, hd2, tq, lambda_init):
    i = pl.program_id(1)
    lam = _lambda_value(lam_ref, lambda_init)
    rows = group * tq
    first_half = lax.broadcasted_iota(jnp.int32, (rows, hd2), 1) < hd2 // 2
    row = lax.broadcasted_iota(jnp.int32, (tq, tq), 0)
    col = lax.broadcasted_iota(jnp.int32, (tq, tq), 1)
    causal = (col <= row)[None, None]
    outs = []
    for n in range(kv_heads):
        qn = q_ref[n * group:(n + 1) * group].reshape(rows, hd2)
        zero = jnp.zeros_like(qn)
        qz = jnp.concatenate([jnp.where(first_half, qn, zero), jnp.where(first_half, zero, qn)], axis=0)
        _softmax_init(m_sc, l_sc, acc_sc)

        def tile(ref, j):
            return ref[pl.ds(pl.multiple_of(j * tq, tq), tq), n * hd2:(n + 1) * hd2]

        def far_step(j, carry):
            _softmax_step(_dot_nt(qz, tile(k_ref, j)), tile(v_ref, j), m_sc, l_sc, acc_sc)
            return carry

        lax.fori_loop(0, jnp.maximum(i - 1, 0), far_step, 0)

        def near_step(j, bias_ref, masked):
            s = _dot_nt(qz, tile(k_ref, j))
            s = s.reshape(2, group, tq, tq) + bias_ref[n * group:(n + 1) * group][None]
            if masked:
                s = jnp.where(causal, s, NEG)
            _softmax_step(s.reshape(2 * rows, tq), tile(v_ref, j), m_sc, l_sc, acc_sc)

        @pl.when(i >= 1)
        def _():
            near_step(i - 1, d1_ref, False)

        near_step(i, d0_ref, True)

        o = acc_sc[...] / jnp.sum(l_sc[...], axis=-1, keepdims=True)
        o = o[:rows] - lam * o[rows:]
        o = _rms(o, gsub_ref[...], SUBLN_EPS) * (1.0 - lambda_init)
        outs.extend(o[g * tq:(g + 1) * tq] for g in range(group))
    attn = jnp.concatenate(outs, axis=-1).astype(BF16)
    o_ref[...] = h_ref[...] + _dot(attn, wo_ref[...])


def _diff_attn_prompt(q, kb, vb, d0, d1, h, lamp, gsub, wo, *, nb, t, tq, kv_heads, lambda_init):
    heads, n, hd2 = q.shape
    dm = h.shape[1]
    nq = t // tq
    group = heads // kv_heads
    nk = kb.shape[1]
    body = functools.partial(_diff_attn_body, kv_heads=kv_heads, group=group, hd2=hd2, tq=tq,
                             lambda_init=lambda_init)
    c2 = lambda b, i: (0, 0)
    c3 = lambda b, i: (0, 0, 0)
    return pl.pallas_call(
        body,
        grid=(nb, nq),
        in_specs=[
            pl.BlockSpec((heads, tq, hd2), lambda b, i: (0, b * nq + i, 0)),
            pl.BlockSpec((t, nk), lambda b, i: (b, 0)),
            pl.BlockSpec((t, nk), lambda b, i: (b, 0)),
            pl.BlockSpec(d0.shape, c3),
            pl.BlockSpec(d1.shape, c3),
            pl.BlockSpec((tq, dm), lambda b, i: (b * nq + i, 0)),
            pl.BlockSpec(lamp.shape, c2),
            pl.BlockSpec(gsub.shape, c2),
            pl.BlockSpec(wo.shape, c2),
        ],
        out_specs=pl.BlockSpec((tq, dm), lambda b, i: (b * nq + i, 0)),
        out_shape=jax.ShapeDtypeStruct((n, dm), F32),
        scratch_shapes=[
            pltpu.VMEM((2 * group * tq, LANES), F32),
            pltpu.VMEM((2 * group * tq, LANES), F32),
            pltpu.VMEM((2 * group * tq, hd2), F32),
        ],
        compiler_params=_cparams("arbitrary", "arbitrary"),
        name="diff_attn_prompt",
    )(q, kb, vb, d0, d1, h, lamp, gsub, wo)


def _proj_resid_body(x_ref, h_ref, w_ref, o_ref):
    o_ref[...] = h_ref[...] + _dot(x_ref[...].astype(BF16), w_ref[...])


def _proj_resid(x, h, w, *, tm):
    n, dm = h.shape
    return pl.pallas_call(
        _proj_resid_body,
        grid=(n // tm,),
        in_specs=[
            pl.BlockSpec((tm, x.shape[1]), lambda i: (i, 0)),
            pl.BlockSpec((tm, dm), lambda i: (i, 0)),
            pl.BlockSpec(w.shape, lambda i: (0, 0)),
        ],
        out_specs=pl.BlockSpec((tm, dm), lambda i: (i, 0)),
        out_shape=jax.ShapeDtypeStruct((n, dm), F32),
        compiler_params=_cparams("arbitrary"),
        name="proj_resid",
    )(x, h, w)


def _new_token_update(qf, k_new, v_new, bias_new, valid, m_sc, l_sc, acc_sc):
    n_new = valid.shape[1]
    lane0 = lax.broadcasted_iota(jnp.int32, (qf.shape[0], LANES), 1) == 0
    s_cols = []
    for t in range(n_new):
        s_t = jnp.sum(qf * k_new[t:t + 1, :], axis=-1, keepdims=True)
        if bias_new is not None:
            s_t = s_t + bias_new[:, t:t + 1]
        s_cols.append(jnp.where(valid[:, t:t + 1], s_t, NEG))
    m_prev = m_sc[...]
    m_new = m_prev
    for s_t in s_cols:
        m_new = jnp.maximum(m_new, s_t)
    alpha = jnp.exp2(m_prev - m_new)
    l = alpha * l_sc[...]
    acc = jnp.tile(alpha, (1, acc_sc.shape[1] // LANES)) * acc_sc[...]
    for t, s_t in enumerate(s_cols):
        p_t = jnp.exp2(s_t - m_new)
        l = l + jnp.where(lane0, p_t, 0.0)
        acc = acc + jnp.tile(p_t, (1, acc_sc.shape[1] // LANES)) * v_new[t:t + 1, :]
    m_sc[...] = m_new
    l_sc[...] = l
    acc_sc[...] = acc


def _mla_decode_body(pt_ref, q_ref, knew_ref, *rest, pages, page, kv_lora, rope, n_new):
    ckv_refs = rest[:pages]
    kr_refs = rest[pages:2 * pages]
    o_ref, cbuf, krbuf, m_sc, l_sc, acc_sc = rest[2 * pages:]
    c = pl.program_id(1)

    @pl.when(c == 0)
    def _():
        _softmax_init(m_sc, l_sc, acc_sc)

    krbuf[rope:, :] = jnp.zeros((krbuf.shape[0] - rope, krbuf.shape[1]), BF16)
    for p in range(pages):
        cbuf[p * page:(p + 1) * page, :] = ckv_refs[p][...].astype(BF16)
        krbuf[:rope, p * page:(p + 1) * page] = kr_refs[p][...].astype(BF16)
    q = q_ref[...]
    ck = cbuf[...]
    s = _dot_nt(q[:, :kv_lora], ck) + _dot(q[:, kv_lora:], krbuf[...])
    _softmax_step(s, ck, m_sc, l_sc, acc_sc)

    @pl.when(c == pl.num_programs(1) - 1)
    def _():
        r = q.shape[0]
        k_new = knew_ref[...]
        t_row = lax.broadcasted_iota(jnp.int32, (r, n_new), 0) % n_new
        t_col = lax.broadcasted_iota(jnp.int32, (r, n_new), 1)
        _new_token_update(q.astype(F32), k_new, k_new[:, :kv_lora], None, t_col <= t_row, m_sc, l_sc, acc_sc)
        o_ref[...] = acc_sc[...] / jnp.sum(l_sc[...], axis=-1, keepdims=True)


def _mla_decode(page_table, q, knew, cache_ckv, cache_kr, *, pages, n_new):
    db, r, kw = q.shape
    n_pages = page_table.shape[1]
    page, kv_lora = cache_ckv.shape[1:]
    rope = cache_kr.shape[1]
    nchunks = n_pages // pages
    pt = page_table.reshape(-1)

    def page_spec(p, shape):
        return pl.BlockSpec((None,) + shape, lambda b, c, pt_ref: (pt_ref[b * n_pages + c * pages + p], 0, 0))

    body = functools.partial(_mla_decode_body, pages=pages, page=page, kv_lora=kv_lora, rope=rope, n_new=n_new)
    grid_spec = pltpu.PrefetchScalarGridSpec(
        num_scalar_prefetch=1,
        grid=(db, nchunks),
        in_specs=[
            pl.BlockSpec((None, r, kw), lambda b, c, pt_ref: (b, 0, 0)),
            pl.BlockSpec((None,) + knew.shape[1:], lambda b, c, pt_ref: (b, 0, 0)),
        ] + [page_spec(p, (page, kv_lora)) for p in range(pages)]
          + [page_spec(p, (rope, page)) for p in range(pages)],
        out_specs=pl.BlockSpec((None, r, kv_lora), lambda b, c, pt_ref: (b, 0, 0)),
        scratch_shapes=[
            pltpu.VMEM((pages * page, kv_lora), BF16),
            pltpu.VMEM((kw - kv_lora, pages * page), BF16),
            pltpu.VMEM((r, LANES), F32),
            pltpu.VMEM((r, LANES), F32),
            pltpu.VMEM((r, kv_lora), F32),
        ],
    )
    return pl.pallas_call(
        body,
        grid_spec=grid_spec,
        out_shape=jax.ShapeDtypeStruct((db, r, kv_lora), F32),
        compiler_params=_cparams("arbitrary", "arbitrary"),
        name="mla_decode",
    )(pt, q, knew, *([cache_ckv] * pages), *([cache_kr] * pages))


def _diff_decode_body(pt_ref, q_ref, knew_ref, vnew_ref, blast_ref, bnew_ref, lam_ref, gsub_ref, *rest,
                      pages, page, kv_heads, group, hd2, n_new, lambda_init):
    k_refs = rest[:pages]
    v_refs = rest[pages:2 * pages]
    o_ref, kbuf, vbuf, s_sc, m_sc, l_sc, acc_sc = rest[2 * pages:]
    c = pl.program_id(1)
    last = pl.num_programs(1) - 1

    @pl.when(c == 0)
    def _():
        _softmax_init(m_sc, l_sc, acc_sc)

    for p in range(pages):
        rows = slice(p * page, (p + 1) * page)
        for n in range(kv_heads):
            lanes = slice(n * hd2, (n + 1) * hd2)
            kbuf[rows, lanes] = k_refs[p][pl.ds(n, page, stride=kv_heads), :].astype(BF16)
            vbuf[rows, lanes] = v_refs[p][pl.ds(n, page, stride=kv_heads), :].astype(BF16)
    q = q_ref[...]
    s_sc[...] = _dot_nt(q, kbuf[...])

    @pl.when(c == last)
    def _():
        cols = slice((pages - 1) * page, pages * page)
        s_sc[:, cols] = s_sc[:, cols] + blast_ref[...]

    _softmax_step(s_sc[...], vbuf[...], m_sc, l_sc, acc_sc)

    @pl.when(c == last)
    def _():
        r = q.shape[0]
        t_row = lax.broadcasted_iota(jnp.int32, (r, n_new), 0) % n_new
        t_col = lax.broadcasted_iota(jnp.int32, (r, n_new), 1)
        _new_token_update(q.astype(F32), knew_ref[...], vnew_ref[...], bnew_ref[...], t_col <= t_row,
                          m_sc, l_sc, acc_sc)
        o = acc_sc[...] / jnp.sum(l_sc[...], axis=-1, keepdims=True)
        lam = _lambda_value(lam_ref, lambda_init)
        gr = group * n_new
        outs = []
        for n in range(kv_heads):
            o1 = o[(2 * n) * gr:(2 * n + 1) * gr, n * hd2:(n + 1) * hd2]
            o2 = o[(2 * n + 1) * gr:(2 * n + 2) * gr, n * hd2:(n + 1) * hd2]
            outs.append(o1 - lam * o2)
        on = jnp.concatenate(outs, axis=0)
        o_ref[...] = _rms(on, gsub_ref[...], SUBLN_EPS) * (1.0 - lambda_init)


def _diff_decode(page_table, q, knew, vnew, blast, bnew, lamp, gsub, cache_k, cache_v, *, pages, kv_heads,
                 group, n_new, lambda_init):
    db, r, nk = q.shape
    n_pages = page_table.shape[1]
    hd2 = nk // kv_heads
    page = cache_k.shape[1] // kv_heads
    nchunks = n_pages // pages
    pt = page_table.reshape(-1)

    def page_spec(p):
        return pl.BlockSpec((None, page * kv_heads, hd2),
                            lambda b, c, pt_ref: (pt_ref[b * n_pages + c * pages + p], 0, 0))

    per_b = lambda b, c, pt_ref: (b, 0, 0)
    const = lambda b, c, pt_ref: (0, 0)
    body = functools.partial(_diff_decode_body, pages=pages, page=page, kv_heads=kv_heads, group=group,
                             hd2=hd2, n_new=n_new, lambda_init=lambda_init)
    r_out = kv_heads * group * n_new
    grid_spec = pltpu.PrefetchScalarGridSpec(
        num_scalar_prefetch=1,
        grid=(db, nchunks),
        in_specs=[
            pl.BlockSpec((None, r, nk), per_b),
            pl.BlockSpec((None,) + knew.shape[1:], per_b),
            pl.BlockSpec((None,) + vnew.shape[1:], per_b),
            pl.BlockSpec(blast.shape, const),
            pl.BlockSpec(bnew.shape, const),
            pl.BlockSpec(lamp.shape, const),
            pl.BlockSpec(gsub.shape, const),
        ] + [page_spec(p) for p in range(pages)] * 2,
        out_specs=pl.BlockSpec((None, r_out, hd2), per_b),
        scratch_shapes=[
            pltpu.VMEM((pages * page, nk), BF16),
            pltpu.VMEM((pages * page, nk), BF16),
            pltpu.VMEM((r, pages * page), F32),
            pltpu.VMEM((r, LANES), F32),
            pltpu.VMEM((r, LANES), F32),
            pltpu.VMEM((r, nk), F32),
        ],
    )
    return pl.pallas_call(
        body,
        grid_spec=grid_spec,
        out_shape=jax.ShapeDtypeStruct((db, r_out, hd2), F32),
        compiler_params=_cparams("arbitrary", "arbitrary"),
        name="diff_decode",
    )(pt, q, knew, vnew, blast, bnew, lamp, gsub, *([cache_k] * pages), *([cache_v] * pages))


def _rope_tables(pos, half, heads):
    inv = ROPE_THETA ** (-jnp.arange(half, dtype=F32) / half)
    ang = pos.astype(F32)[:, None] * inv[None, :]
    cos, sin = jnp.cos(ang), jnp.sin(ang)
    cos2 = jnp.concatenate([cos, cos], axis=-1)
    sin2 = jnp.concatenate([-sin, sin], axis=-1)
    return jnp.tile(cos2, (1, heads)), jnp.tile(sin2, (1, heads))


def _bucket_of_distance(n):
    max_exact = REL_BUCKETS // 2
    nf = np.maximum(n, 1).astype(np.float32)
    ratio = np.log(nf / np.float32(max_exact)) / np.float32(math.log(REL_MAX_DIST / max_exact))
    large = max_exact + (ratio * np.float32(REL_BUCKETS - max_exact)).astype(np.int32)
    large = np.minimum(large, REL_BUCKETS - 1)
    return np.where(n < max_exact, n, large)


def _bias_by_distance(rel_bias, n, far_from, max_dist):
    far = _bucket_of_distance(np.arange(far_from, max_dist + 1))
    assert (far == far[0]).all(), "relative-position bias must be constant beyond the near tiles"
    select = np.zeros((n, REL_BUCKETS), np.float32)
    select[np.arange(n), _bucket_of_distance(np.arange(n))] = 1.0
    select[:, int(far[0])] -= 1.0
    picked = jnp.dot(jnp.asarray(select), rel_bias.astype(F32), precision=lax.Precision.HIGHEST)
    return LOG2E * picked.T


def _toeplitz(v, n, offset):
    heads = v.shape[0]
    length = 2 * n
    lo = n - 1 - offset
    u = jnp.pad(v, ((0, 0), (lo, 0)))[:, :length] if lo >= 0 else jnp.pad(v[:, -lo:], ((0, 0), (0, -lo)))
    x = jnp.tile(u, (1, n + 1))[:, :n * (length + 1)].reshape(heads, n, length + 1)[:, :, :n]
    return x[:, :, ::-1]


def kernel(x_prompt, x_sample, cache_mla_ckv, cache_mla_krope, cache_diff_k, cache_diff_v, page_table, g_mix, g_ffn, g_final, w_mla_down, g_mla_q, g_mla_kv, w_mla_uq, w_mla_uk, w_mla_uv, w_mla_o, w_diff_qkv, lam_q1, lam_k1, lam_q2, lam_k2, g_diff_sub, w_diff_o, rel_bias, w_ff_up, w_ff_down):
    nb, t, dm = x_prompt.shape
    db, n_new, _ = x_sample.shape
    n_pages = page_table.shape[1]
    page = cache_mla_ckv.shape[2]
    past = n_pages * page
    kv_lora, heads, nope = w_mla_uk.shape[1:]
    q_lora = g_mla_q.shape[1]
    rope = cache_mla_krope.shape[3]
    kv_heads, hd2 = cache_diff_k.shape[3:]
    nk = kv_heads * hd2
    dheads = rel_bias.shape[1]
    group = dheads // kv_heads
    assert g_mix.shape[0] == 2 and w_mla_down.shape[0] == 1 and w_diff_qkv.shape[0] == 1
    assert heads * rope == 4 * LANES and kv_lora % LANES == 0 and hd2 == LANES and rope * 2 == LANES

    tm_p = min(ROW_TILE, t)
    ns = db * n_new
    tm_s = min(ROW_TILE, ns)
    tq = min(ATTN_TILE, t)
    pages = min(PAGES_PER_STEP, n_pages)
    assert t % tm_p == 0 and ns % tm_s == 0 and t % tq == 0 and n_pages % pages == 0
    lambda_init = 0.8 - 0.6 * math.exp(-0.3 * 1)

    row = lambda v: v.reshape(1, -1).astype(F32)
    bf = lambda w: w.astype(BF16)

    wd = w_mla_down[0]
    r0 = q_lora + kv_lora
    half = rope // 2
    wd_ext = bf(jnp.concatenate([wd, wd[:, r0 + half:r0 + rope], wd[:, r0:r0 + half]], axis=1))
    wuq = w_mla_uq[0].reshape(q_lora, heads, nope + rope)
    wuq_ext = bf(jnp.concatenate([
        wuq[:, :, :nope].reshape(q_lora, heads * nope),
        wuq[:, :, nope:].reshape(q_lora, heads * rope),
        jnp.concatenate([wuq[:, :, nope + half:], wuq[:, :, nope:nope + half]], axis=-1).reshape(q_lora, heads * rope),
    ], axis=1))
    wuk_t = bf(jnp.transpose(w_mla_uk[0], (1, 2, 0)))
    wuv = bf(jnp.transpose(w_mla_uv[0], (1, 0, 2)))
    wo_mla = bf(w_mla_o[0])
    wqkv = bf(w_diff_qkv[0])
    wo_diff = bf(w_diff_o[0])
    wup = bf(w_ff_up)
    wdn = bf(w_ff_down)
    lamp = jnp.stack([lam_q1[0], lam_k1[0], lam_q2[0], lam_k2[0]]).astype(F32)
    gsub = row(g_diff_sub[0])
    mla_dims = (heads, q_lora, kv_lora, nope, rope)

    cos_p, sin_p = _rope_tables(jnp.arange(t, dtype=jnp.int32), half, heads)
    pos_s = past + (jnp.arange(ns, dtype=jnp.int32) % n_new)
    cos_s, sin_s = _rope_tables(pos_s, half, heads)
    bias_p = _bias_by_distance(rel_bias, 2 * tq, tq + 1, t + tq)
    d0 = _toeplitz(bias_p, tq, 0)
    d1 = _toeplitz(bias_p, tq, tq)

    hp = x_prompt.reshape(nb * t, dm)
    hs = x_sample.reshape(ns, dm)

    c_p, krt_p, kcat_p, q_p = _mla_pre(hp, row(g_mix[0]), wd_ext, row(g_mla_q[0]), row(g_mla_kv[0]), wuq_ext,
                                       wuk_t, cos_p, sin_p, nb=nb, tm=tm_p, dims=mla_dims)
    hp = _mla_attn_prompt(q_p, kcat_p, hp, wuv, wo_mla, nb=nb, t=t, tq=tq, kv_lora=kv_lora)
    hp = _ffn(hp, row(g_ffn[0]), wup[0], wdn[0], row(g_final), tm=tm_p, final_norm=False)

    c_s, krt_s, _, q_s = _mla_pre(hs, row(g_mix[0]), wd_ext, row(g_mla_q[0]), row(g_mla_kv[0]), wuq_ext,
                                  wuk_t, cos_s, sin_s, nb=1, tm=tm_s, dims=mla_dims)
    kw = kv_lora + LANES
    q_dec = q_s.reshape(heads, db, n_new, kw).transpose(1, 0, 2, 3).reshape(db, heads * n_new, kw)
    kr_s = krt_s[0].T
    knew = jnp.concatenate([c_s, kr_s, jnp.zeros((ns, LANES - rope), F32)], axis=-1).reshape(db, n_new, kw)
    knew = jnp.pad(knew, ((0, 0), (0, 8 - n_new), (0, 0)))
    o_lat = _mla_decode(page_table, q_dec, knew, cache_mla_ckv[0], jnp.swapaxes(cache_mla_krope[0], 1, 2),
                        pages=pages, n_new=n_new)
    o_lat = o_lat.reshape(db, heads, n_new, kv_lora).transpose(0, 2, 1, 3).reshape(ns, heads * kv_lora)
    hs = _mla_post(o_lat, hs, wuv, wo_mla, tm=tm_s)
    hs = _ffn(hs, row(g_ffn[0]), wup[0], wdn[0], row(g_final), tm=tm_s, final_norm=False)

    qd_p, k_p, v_p, kb_p, vb_p = _diff_pre(hp, row(g_mix[1]), wqkv, tm=tm_p, heads=dheads, hd2=hd2, nk=nk)
    hp = _diff_attn_prompt(qd_p, kb_p, vb_p, d0, d1, hp, lamp, gsub, wo_diff, nb=nb, t=t, tq=tq,
                           kv_heads=kv_heads, lambda_init=lambda_init)
    y_p = _ffn(hp, row(g_ffn[1]), wup[1], wdn[1], row(g_final), tm=tm_p, final_norm=True)

    qd_s, k_s, v_s, _, _ = _diff_pre(hs, row(g_mix[1]), wqkv, tm=tm_s, heads=dheads, hd2=hd2, nk=nk)
    q6 = qd_s.reshape(kv_heads, group, db, n_new, 2, hd2 // 2).transpose(2, 0, 4, 1, 3, 5)
    place = jnp.eye(2 * kv_heads, dtype=BF16).reshape(kv_heads, 2, 2 * kv_heads)
    q_dec = (q6[..., None, :] * place[None, :, :, None, None, :, None]).reshape(db, 2 * dheads * n_new, nk)
    pad_new = lambda a: jnp.pad(a.reshape(db, n_new, nk), ((0, 0), (0, 8 - n_new), (0, 0)))
    bias_s = _bias_by_distance(rel_bias, page + n_new, page + 1, past + n_new)
    b_last = jnp.stack([bias_s[:, tk + 1:tk + 1 + page][:, ::-1] for tk in range(n_new)], axis=1)
    b_new = jnp.stack([jnp.pad(bias_s[:, :tk + 1][:, ::-1], ((0, 0), (0, LANES - tk - 1))) for tk in range(n_new)],
                      axis=1)

    def per_row(b):
        b = jnp.broadcast_to(b.reshape(kv_heads, 1, group, n_new, -1), (kv_heads, 2, group, n_new, b.shape[-1]))
        return b.reshape(2 * dheads * n_new, -1)

    b_last, b_new = per_row(b_last), per_row(b_new)
    o_d = _diff_decode(page_table, q_dec, pad_new(k_s), pad_new(v_s), b_last, b_new, lamp, gsub,
                       cache_diff_k[0].reshape(-1, page * kv_heads, hd2),
                       cache_diff_v[0].reshape(-1, page * kv_heads, hd2),
                       pages=pages, kv_heads=kv_heads, group=group, n_new=n_new, lambda_init=lambda_init)
    o_d = o_d.reshape(db, kv_heads, group, n_new, hd2).transpose(0, 3, 1, 2, 4).reshape(ns, dheads * hd2)
    hs = _proj_resid(o_d, hs, wo_diff, tm=tm_s)
    y_s = _ffn(hs, row(g_ffn[1]), wup[1], wdn[1], row(g_final), tm=tm_s, final_norm=True)

    return (y_p.reshape(nb, t, dm), y_s.reshape(db, n_new, dm),
            c_p.reshape(1, nb, t, kv_lora), jnp.swapaxes(krt_p, 1, 2)[None],
            k_p.reshape(1, nb, t, kv_heads, hd2), v_p.reshape(1, nb, t, kv_heads, hd2),
            c_s.reshape(1, db, n_new, kv_lora), kr_s.reshape(1, db, n_new, rope),
            k_s.reshape(1, db, n_new, kv_heads, hd2), v_s.reshape(1, db, n_new, kv_heads, hd2))
```

```python
import functools
import math

import numpy as np
import jax
import jax.numpy as jnp
from jax import lax
from jax.experimental import pallas as pl
from jax.experimental.pallas import tpu as pltpu

F32 = jnp.float32
BF16 = jnp.bfloat16

ROPE_THETA = 10000.0
REL_BUCKETS = 32
REL_MAX_DIST = 128
NORM_EPS = 1e-6
SUBLN_EPS = 1e-5
NEG = -1e30
LOG2E = math.log2(math.e)

LANES = 128
ROW_TILE = 512
ATTN_TILE = 256
PAGES_PER_STEP = 32
VMEM_LIMIT = 56 * 1024 * 1024


def _cparams(*sem):
    return pltpu.CompilerParams(dimension_semantics=sem, vmem_limit_bytes=VMEM_LIMIT)


def _rms(x, g, eps):
    return x * lax.rsqrt(jnp.mean(x * x, axis=-1, keepdims=True) + eps) * g


def _dot(a, b):
    return jnp.dot(a, b, preferred_element_type=F32)


def _dot_nt(a, b):
    return lax.dot_general(a, b, (((1,), (1,)), ((), ())), preferred_element_type=F32)


def _lane_block_sum(p):
    out = p[:, :LANES]
    for j in range(1, p.shape[1] // LANES):
        out = out + p[:, j * LANES:(j + 1) * LANES]
    return out


def _softmax_step(s, v, m_sc, l_sc, acc_sc):
    m_prev = m_sc[...]
    m_new = jnp.maximum(m_prev, jnp.max(s, axis=-1, keepdims=True))
    alpha = jnp.exp2(m_prev - m_new)
    p = jnp.exp2(s - jnp.tile(m_new, (1, s.shape[1] // LANES)))
    l_sc[...] = alpha * l_sc[...] + _lane_block_sum(p)
    acc_sc[...] = jnp.tile(alpha, (1, acc_sc.shape[1] // LANES)) * acc_sc[...] + _dot(p.astype(BF16), v)
    m_sc[...] = m_new


def _softmax_init(m_sc, l_sc, acc_sc):
    m_sc[...] = jnp.full(m_sc.shape, NEG, F32)
    l_sc[...] = jnp.zeros(l_sc.shape, F32)
    acc_sc[...] = jnp.zeros(acc_sc.shape, F32)


def _mla_pre_body(h_ref, g_ref, wd_ref, gq_ref, gkv_ref, wuq_ref, wuk_ref, cos_ref, sin_ref,
                  c_ref, kr_ref, kcat_ref, q_ref, *, scale, heads, q_lora, kv_lora, nope, rope):
    tm = h_ref.shape[0]
    a = _rms(h_ref[...], g_ref[...], NORM_EPS).astype(BF16)
    d = _dot(a, wd_ref[...])
    cq = _rms(d[:, :q_lora], gq_ref[...], NORM_EPS)
    c = _rms(d[:, q_lora:q_lora + kv_lora], gkv_ref[...], NORM_EPS)
    cosq = cos_ref[...]
    sinq = sin_ref[...]
    r0 = q_lora + kv_lora
    kr = d[:, r0:r0 + rope] * cosq[:, :rope] + d[:, r0 + rope:r0 + 2 * rope] * sinq[:, :rope]
    c_ref[...] = c
    pad = jnp.zeros((tm, LANES - rope), F32)
    kr_ref[...] = jnp.concatenate([kr, pad], axis=-1).T[:rope]
    kcat_ref[...] = jnp.concatenate([c, kr, pad], axis=-1).astype(BF16)
    q = _dot(cq.astype(BF16), wuq_ref[...])
    hn = heads * nope
    hr = heads * rope
    qr = (q[:, hn:hn + hr] * cosq + q[:, hn + hr:hn + 2 * hr] * sinq) * scale
    for h in range(heads):
        ql = _dot(q[:, h * nope:(h + 1) * nope].astype(BF16), wuk_ref[h]) * scale
        q_ref[h] = jnp.concatenate([ql, qr[:, h * rope:(h + 1) * rope], pad], axis=-1).astype(BF16)


def _mla_pre(h, g, wd, gq, gkv, wuq, wuk, cosq, sinq, *, nb, tm, dims):
    n, dm = h.shape
    t = cosq.shape[0]
    nt = t // tm
    heads, q_lora, kv_lora, nope, rope = dims
    kw = kv_lora + LANES
    row = lambda ti, b: (b * nt + ti, 0)
    const2 = lambda ti, b: (0, 0)
    body = functools.partial(_mla_pre_body, scale=LOG2E * (nope + rope) ** -0.5, heads=heads, q_lora=q_lora,
                             kv_lora=kv_lora, nope=nope, rope=rope)
    return pl.pallas_call(
        body,
        grid=(nt, nb),
        in_specs=[
            pl.BlockSpec((tm, dm), row),
            pl.BlockSpec(g.shape, const2),
            pl.BlockSpec(wd.shape, const2),
            pl.BlockSpec(gq.shape, const2),
            pl.BlockSpec(gkv.shape, const2),
            pl.BlockSpec(wuq.shape, const2),
            pl.BlockSpec(wuk.shape, lambda ti, b: (0, 0, 0)),
            pl.BlockSpec((tm, cosq.shape[1]), lambda ti, b: (ti, 0)),
            pl.BlockSpec((tm, sinq.shape[1]), lambda ti, b: (ti, 0)),
        ],
        out_specs=[
            pl.BlockSpec((tm, kv_lora), row),
            pl.BlockSpec((None, rope, tm), lambda ti, b: (b, 0, ti)),
            pl.BlockSpec((tm, kw), row),
            pl.BlockSpec((heads, tm, kw), lambda ti, b: (0, b * nt + ti, 0)),
        ],
        out_shape=[
            jax.ShapeDtypeStruct((n, kv_lora), F32),
            jax.ShapeDtypeStruct((nb, rope, t), F32),
            jax.ShapeDtypeStruct((n, kw), BF16),
            jax.ShapeDtypeStruct((heads, n, kw), BF16),
        ],
        compiler_params=_cparams("arbitrary", "arbitrary"),
        name="mla_pre",
    )(h, g, wd, gq, gkv, wuq, wuk, cosq, sinq)


def _mla_out(o_heads, wuv_ref, wo_ref, resid):
    ov = [_dot(o.astype(BF16), wuv_ref[h]) for h, o in enumerate(o_heads)]
    o = jnp.concatenate(ov, axis=-1).astype(BF16)
    return resid + _dot(o, wo_ref[...])


def _mla_attn_body(q_ref, k_ref, h_ref, wuv_ref, wo_ref, o_ref, m_sc, l_sc, acc_sc, *, heads, kv_lora, tq):
    i = pl.program_id(1)
    q = q_ref[...].reshape(heads * tq, q_ref.shape[2])
    _softmax_init(m_sc, l_sc, acc_sc)

    def keys(j):
        return k_ref[pl.ds(pl.multiple_of(j * tq, tq), tq), :]

    def full_step(j, s):
        s_next = _dot_nt(q, keys(j + 1))
        _softmax_step(s, keys(j)[:, :kv_lora], m_sc, l_sc, acc_sc)
        return s_next

    s = lax.fori_loop(0, i, full_step, _dot_nt(q, keys(0)))
    row = lax.broadcasted_iota(jnp.int32, (tq, tq), 0)
    col = lax.broadcasted_iota(jnp.int32, (tq, tq), 1)
    s = jnp.where((col <= row)[None], s.reshape(heads, tq, tq), NEG).reshape(heads * tq, tq)
    _softmax_step(s, keys(i)[:, :kv_lora], m_sc, l_sc, acc_sc)

    o = acc_sc[...] / jnp.sum(l_sc[...], axis=-1, keepdims=True)
    o_heads = [o[h * tq:(h + 1) * tq] for h in range(heads)]
    o_ref[...] = _mla_out(o_heads, wuv_ref, wo_ref, h_ref[...])


def _mla_attn_prompt(q, kcat, h, wuv, wo, *, nb, t, tq, kv_lora):
    heads, n, kw = q.shape
    dm = h.shape[1]
    nq = t // tq
    body = functools.partial(_mla_attn_body, heads=heads, kv_lora=kv_lora, tq=tq)
    return pl.pallas_call(
        body,
        grid=(nb, nq),
        in_specs=[
            pl.BlockSpec((heads, tq, kw), lambda b, i: (0, b * nq + i, 0)),
            pl.BlockSpec((t, kw), lambda b, i: (b, 0)),
            pl.BlockSpec((tq, dm), lambda b, i: (b * nq + i, 0)),
            pl.BlockSpec(wuv.shape, lambda b, i: (0, 0, 0)),
            pl.BlockSpec(wo.shape, lambda b, i: (0, 0)),
        ],
        out_specs=pl.BlockSpec((tq, dm), lambda b, i: (b * nq + i, 0)),
        out_shape=jax.ShapeDtypeStruct((n, dm), F32),
        scratch_shapes=[
            pltpu.VMEM((heads * tq, LANES), F32),
            pltpu.VMEM((heads * tq, LANES), F32),
            pltpu.VMEM((heads * tq, kv_lora), F32),
        ],
        compiler_params=_cparams("arbitrary", "arbitrary"),
        name="mla_attn_prompt",
    )(q, kcat, h, wuv, wo)


def _mla_post_body(o_ref, h_ref, wuv_ref, wo_ref, out_ref, *, heads, kv_lora):
    o = o_ref[...]
    o_heads = [o[:, h * kv_lora:(h + 1) * kv_lora] for h in range(heads)]
    out_ref[...] = _mla_out(o_heads, wuv_ref, wo_ref, h_ref[...])


def _mla_post(o_lat, h, wuv, wo, *, tm):
    n, dm = h.shape
    heads, kv_lora, _ = wuv.shape
    body = functools.partial(_mla_post_body, heads=heads, kv_lora=kv_lora)
    return pl.pallas_call(
        body,
        grid=(n // tm,),
        in_specs=[
            pl.BlockSpec((tm, o_lat.shape[1]), lambda i: (i, 0)),
            pl.BlockSpec((tm, dm), lambda i: (i, 0)),
            pl.BlockSpec(wuv.shape, lambda i: (0, 0, 0)),
            pl.BlockSpec(wo.shape, lambda i: (0, 0)),
        ],
        out_specs=pl.BlockSpec((tm, dm), lambda i: (i, 0)),
        out_shape=jax.ShapeDtypeStruct((n, dm), F32),
        compiler_params=_cparams("arbitrary"),
        name="mla_post",
    )(o_lat, h, wuv, wo)


def _ffn_body(h_ref, g_ref, wup_ref, wdn_ref, gf_ref, o_ref, *, chunk, final_norm):
    x = h_ref[...]
    a = _rms(x, g_ref[...], NORM_EPS).astype(BF16)
    acc = x
    for j in range(wup_ref.shape[1] // chunk):
        u = jnp.maximum(_dot(a, wup_ref[:, j * chunk:(j + 1) * chunk]), 0.0)
        acc = acc + _dot((u * u).astype(BF16), wdn_ref[j * chunk:(j + 1) * chunk, :])
    if final_norm:
        acc = _rms(acc, gf_ref[...], NORM_EPS)
    o_ref[...] = acc


def _ffn(h, g, wup, wdn, gf, *, tm, final_norm):
    n, dm = h.shape
    body = functools.partial(_ffn_body, chunk=min(1024, wup.shape[1]), final_norm=final_norm)
    const = lambda i: (0, 0)
    return pl.pallas_call(
        body,
        grid=(n // tm,),
        in_specs=[
            pl.BlockSpec((tm, dm), lambda i: (i, 0)),
            pl.BlockSpec(g.shape, const),
            pl.BlockSpec(wup.shape, const, pipeline_mode=pl.Buffered(1)),
            pl.BlockSpec(wdn.shape, const, pipeline_mode=pl.Buffered(1)),
            pl.BlockSpec(gf.shape, const),
        ],
        out_specs=pl.BlockSpec((tm, dm), lambda i: (i, 0)),
        out_shape=jax.ShapeDtypeStruct((n, dm), F32),
        compiler_params=_cparams("arbitrary"),
        name="ffn_final" if final_norm else "ffn",
    )(h, g, wup, wdn, gf)


def _diff_pre_body(h_ref, g_ref, w_ref, q_ref, k_ref, v_ref, kb_ref, vb_ref, *, scale, heads, hd2, nk):
    a = _rms(h_ref[...], g_ref[...], NORM_EPS).astype(BF16)
    proj = _dot(a, w_ref[...])
    nq = heads * hd2
    for h in range(heads):
        q_ref[h] = (proj[:, h * hd2:(h + 1) * hd2] * scale).astype(BF16)
    k = proj[:, nq:nq + nk]
    v = proj[:, nq + nk:nq + 2 * nk]
    kv_heads = nk // hd2
    tm = h_ref.shape[0]
    for n in range(kv_heads):
        k_ref[pl.ds(n, tm, stride=kv_heads), :] = k[:, n * hd2:(n + 1) * hd2]
        v_ref[pl.ds(n, tm, stride=kv_heads), :] = v[:, n * hd2:(n + 1) * hd2]
    kb_ref[...] = k.astype(BF16)
    vb_ref[...] = v.astype(BF16)


def _diff_pre(h, g, w, *, tm, heads, hd2, nk):
    n, dm = h.shape
    body = functools.partial(_diff_pre_body, scale=LOG2E * (hd2 // 2) ** -0.5, heads=heads, hd2=hd2, nk=nk)
    row = lambda i: (i, 0)
    return pl.pallas_call(
        body,
        grid=(n // tm,),
        in_specs=[
            pl.BlockSpec((tm, dm), row),
            pl.BlockSpec(g.shape, lambda i: (0, 0)),
            pl.BlockSpec(w.shape, lambda i: (0, 0)),
        ],
        out_specs=[
            pl.BlockSpec((heads, tm, hd2), lambda i: (0, i, 0)),
            pl.BlockSpec((tm * nk // hd2, hd2), row),
            pl.BlockSpec((tm * nk // hd2, hd2), row),
            pl.BlockSpec((tm, nk), row),
            pl.BlockSpec((tm, nk), row),
        ],
        out_shape=[
            jax.ShapeDtypeStruct((heads, n, hd2), BF16),
            jax.ShapeDtypeStruct((n * nk // hd2, hd2), F32),
            jax.ShapeDtypeStruct((n * nk // hd2, hd2), F32),
            jax.ShapeDtypeStruct((n, nk), BF16),
            jax.ShapeDtypeStruct((n, nk), BF16),
        ],
        compiler_params=_cparams("arbitrary"),
        name="diff_pre",
    )(h, g, w)


def _lambda_value(lam_ref, lambda_init):
    lp = lam_ref[...]
    e1 = jnp.exp(jnp.sum(lp[0:1] * lp[1:2], axis=-1, keepdims=True))
    e2 = jnp.exp(jnp.sum(lp[2:3] * lp[3:4], axis=-1, keepdims=True))
    return e1 - e2 + lambda_init


def _diff_attn_body(q_ref, k_ref, v_ref, d0_ref, d1_ref, h_ref, lam_ref, gsub_ref, wo_ref, o_ref,
                    m_sc, l_sc, acc_sc, *, kv_heads, group, hd2, tq, lambda_init):
    i = pl.program_id(1)
    lam = _lambda_value(lam_ref, lambda_init)
    rows = group * tq
    first_half = lax.broadcasted_iota(jnp.int32, (rows, hd2), 1) < hd2 // 2
    row = lax.broadcasted_iota(jnp.int32, (tq, tq), 0)
    col = lax.broadcasted_iota(jnp.int32, (tq, tq), 1)
    causal = (col <= row)[None, None]
    outs = []
    for n in range(kv_heads):
        qn = q_ref[n * group:(n + 1) * group].reshape(rows, hd2)
        zero = jnp.zeros_like(qn)
        qz = jnp.concatenate([jnp.where(first_half, qn, zero), jnp.where(first_half, zero, qn)], axis=0)
        _softmax_init(m_sc, l_sc, acc_sc)

        def tile(ref, j):
            return ref[pl.ds(pl.multiple_of(j * tq, tq), tq), n * hd2:(n + 1) * hd2]

        def far_step(j, carry):
            _softmax_step(_dot_nt(qz, tile(k_ref, j)), tile(v_ref, j), m_sc, l_sc, acc_sc)
            return carry

        lax.fori_loop(0, jnp.maximum(i - 1, 0), far_step, 0)

        def near_step(j, bias_ref, masked):
            s = _dot_nt(qz, tile(k_ref, j))
            s = s.reshape(2, group, tq, tq) + bias_ref[n * group:(n + 1) * group][None]
            if masked:
                s = jnp.where(causal, s, NEG)
            _softmax_step(s.reshape(2 * rows, tq), tile(v_ref, j), m_sc, l_sc, acc_sc)

        @pl.when(i >= 1)
        def _():
            near_step(i - 1, d1_ref, False)

        near_step(i, d0_ref, True)

        o = acc_sc[...] / jnp.sum(l_sc[...], axis=-1, keepdims=True)
        o = o[:rows] - lam * o[rows:]
        o = _rms(o, gsub_ref[...], SUBLN_EPS) * (1.0 - lambda_init)
        outs.extend(o[g * tq:(g + 1) * tq] for g in range(group))
    attn = jnp.concatenate(outs, axis=-1).astype(BF16)
    o_ref[...] = h_ref[...] + _dot(attn, wo_ref[...])


def _diff_attn_prompt(q, kb, vb, d0, d1, h, lamp, gsub, wo, *, nb, t, tq, kv_heads, lambda_init):
    heads, n, hd2 = q.shape
    dm = h.shape[1]
    nq = t // tq
    group = heads // kv_heads
    nk = kb.shape[1]
    body = functools.partial(_diff_attn_body, kv_heads=kv_heads, group=group, hd2=hd2, tq=tq,
                             lambda_init=lambda_init)
    c2 = lambda b, i: (0, 0)
    c3 = lambda b, i: (0, 0, 0)
    return pl.pallas_call(
        body,
        grid=(nb, nq),
        in_specs=[
            pl.BlockSpec((heads, tq, hd2), lambda b, i: (0, b * nq + i, 0)),
            pl.BlockSpec((t, nk), lambda b, i: (b, 0)),
            pl.BlockSpec((t, nk), lambda b, i: (b, 0)),
            pl.BlockSpec(d0.shape, c3),
            pl.BlockSpec(d1.shape, c3),
            pl.BlockSpec((tq, dm), lambda b, i: (b * nq + i, 0)),
            pl.BlockSpec(lamp.shape, c2),
            pl.BlockSpec(gsub.shape, c2),
            pl.BlockSpec(wo.shape, c2),
        ],
        out_specs=pl.BlockSpec((tq, dm), lambda b, i: (b * nq + i, 0)),
        out_shape=jax.ShapeDtypeStruct((n, dm), F32),
        scratch_shapes=[
            pltpu.VMEM((2 * group * tq, LANES), F32),
            pltpu.VMEM((2 * group * tq, LANES), F32),
            pltpu.VMEM((2 * group * tq, hd2), F32),
        ],
        compiler_params=_cparams("arbitrary", "arbitrary"),
        name="diff_attn_prompt",
    )(q, kb, vb, d0, d1, h, lamp, gsub, wo)


def _proj_resid_body(x_ref, h_ref, w_ref, o_ref):
    o_ref[...] = h_ref[...] + _dot(x_ref[...].astype(BF16), w_ref[...])


def _proj_resid(x, h, w, *, tm):
    n, dm = h.shape
    return pl.pallas_call(
        _proj_resid_body,
        grid=(n // tm,),
        in_specs=[
            pl.BlockSpec((tm, x.shape[1]), lambda i: (i, 0)),
            pl.BlockSpec((tm, dm), lambda i: (i, 0)),
            pl.BlockSpec(w.shape, lambda i: (0, 0)),
        ],
        out_specs=pl.BlockSpec((tm, dm), lambda i: (i, 0)),
        out_shape=jax.ShapeDtypeStruct((n, dm), F32),
        compiler_params=_cparams("arbitrary"),
        name="proj_resid",
    )(x, h, w)


def _new_token_update(qf, k_new, v_new, bias_new, valid, m_sc, l_sc, acc_sc):
    n_new = valid.shape[1]
    lane0 = lax.broadcasted_iota(jnp.int32, (qf.shape[0], LANES), 1) == 0
    s_cols = []
    for t in range(n_new):
        s_t = jnp.sum(qf * k_new[t:t + 1, :], axis=-1, keepdims=True)
        if bias_new is not None:
            s_t = s_t + bias_new[:, t:t + 1]
        s_cols.append(jnp.where(valid[:, t:t + 1], s_t, NEG))
    m_prev = m_sc[...]
    m_new = m_prev
    for s_t in s_cols:
        m_new = jnp.maximum(m_new, s_t)
    alpha = jnp.exp2(m_prev - m_new)
    l = alpha * l_sc[...]
    acc = jnp.tile(alpha, (1, acc_sc.shape[1] // LANES)) * acc_sc[...]
    for t, s_t in enumerate(s_cols):
        p_t = jnp.exp2(s_t - m_new)
        l = l + jnp.where(lane0, p_t, 0.0)
        acc = acc + jnp.tile(p_t, (1, acc_sc.shape[1] // LANES)) * v_new[t:t + 1, :]
    m_sc[...] = m_new
    l_sc[...] = l
    acc_sc[...] = acc


def _page_ring(pt_ref, srcs, bufs, sems, *, pages):
    nchunks = pl.num_programs(1)
    step = pl.program_id(0) * nchunks + pl.program_id(1)
    total = pl.num_programs(0) * nchunks
    slot = lax.rem(step, 2)

    def copies(step_idx, slot_idx):
        out = []
        for p in range(pages):
            pg = pt_ref[step_idx * pages + p]
            for src, buf, sem in zip(srcs, bufs, sems):
                out.append(pltpu.make_async_copy(src.at[pg], buf.at[slot_idx, p], sem.at[slot_idx]))
        return out

    @pl.when(step == 0)
    def _():
        for cp in copies(step, slot):
            cp.start()

    @pl.when(step + 1 < total)
    def _():
        for cp in copies(step + 1, 1 - slot):
            cp.start()

    for cp in copies(step, slot):
        cp.wait()
    return slot


def _mla_decode_body(pt_ref, q_ref, knew_ref, ckv_hbm, kr_hbm, o_ref, ckv_pg, kr_pg, cbuf, krbuf, m_sc, l_sc, acc_sc,
                     sem_c, sem_r, *, pages, page, kv_lora, rope, n_new):
    c = pl.program_id(1)
    slot = _page_ring(pt_ref, (ckv_hbm, kr_hbm), (ckv_pg, kr_pg), (sem_c, sem_r), pages=pages)

    @pl.when(c == 0)
    def _():
        _softmax_init(m_sc, l_sc, acc_sc)

    krbuf[rope:, :] = jnp.zeros((krbuf.shape[0] - rope, krbuf.shape[1]), BF16)
    for p in range(pages):
        cbuf[p * page:(p + 1) * page, :] = ckv_pg[slot, p].astype(BF16)
        krbuf[:rope, p * page:(p + 1) * page] = kr_pg[slot, p].astype(BF16)
    q = q_ref[...]
    ck = cbuf[...]
    s = _dot_nt(q[:, :kv_lora], ck) + _dot(q[:, kv_lora:], krbuf[...])
    _softmax_step(s, ck, m_sc, l_sc, acc_sc)

    @pl.when(c == pl.num_programs(1) - 1)
    def _():
        r = q.shape[0]
        k_new = knew_ref[...]
        t_row = lax.broadcasted_iota(jnp.int32, (r, n_new), 0) % n_new
        t_col = lax.broadcasted_iota(jnp.int32, (r, n_new), 1)
        _new_token_update(q.astype(F32), k_new, k_new[:, :kv_lora], None, t_col <= t_row, m_sc, l_sc, acc_sc)
        o_ref[...] = acc_sc[...] / jnp.sum(l_sc[...], axis=-1, keepdims=True)


def _mla_decode(page_table, q, knew, cache_ckv, cache_kr, *, pages, n_new):
    db, r, kw = q.shape
    n_pages = page_table.shape[1]
    page, kv_lora = cache_ckv.shape[1:]
    rope = cache_kr.shape[1]
    nchunks = n_pages // pages
    pt = page_table.reshape(-1)

    body = functools.partial(_mla_decode_body, pages=pages, page=page, kv_lora=kv_lora, rope=rope, n_new=n_new)
    grid_spec = pltpu.PrefetchScalarGridSpec(
        num_scalar_prefetch=1,
        grid=(db, nchunks),
        in_specs=[
            pl.BlockSpec((None, r, kw), lambda b, c, pt_ref: (b, 0, 0)),
            pl.BlockSpec((None,) + knew.shape[1:], lambda b, c, pt_ref: (b, 0, 0)),
            pl.BlockSpec(memory_space=pl.ANY),
            pl.BlockSpec(memory_space=pl.ANY),
        ],
        out_specs=pl.BlockSpec((None, r, kv_lora), lambda b, c, pt_ref: (b, 0, 0)),
        scratch_shapes=[
            pltpu.VMEM((2, pages, page, kv_lora), F32),
            pltpu.VMEM((2, pages, rope, page), F32),
            pltpu.VMEM((pages * page, kv_lora), BF16),
            pltpu.VMEM((kw - kv_lora, pages * page), BF16),
            pltpu.VMEM((r, LANES), F32),
            pltpu.VMEM((r, LANES), F32),
            pltpu.VMEM((r, kv_lora), F32),
            pltpu.SemaphoreType.DMA((2,)),
            pltpu.SemaphoreType.DMA((2,)),
        ],
    )
    return pl.pallas_call(
        body,
        grid_spec=grid_spec,
        out_shape=jax.ShapeDtypeStruct((db, r, kv_lora), F32),
        compiler_params=_cparams("arbitrary", "arbitrary"),
        name="mla_decode",
    )(pt, q, knew, cache_ckv, cache_kr)


def _diff_decode_body(pt_ref, q_ref, knew_ref, vnew_ref, blast_ref, bnew_ref, lam_ref, gsub_ref, k_hbm, v_hbm,
                      o_ref, k_pg, v_pg, kbuf, vbuf, s_sc, m_sc, l_sc, acc_sc, sem_k, sem_v, *,
                      pages, page, kv_heads, group, hd2, n_new, lambda_init):
    c = pl.program_id(1)
    last = pl.num_programs(1) - 1
    slot = _page_ring(pt_ref, (k_hbm, v_hbm), (k_pg, v_pg), (sem_k, sem_v), pages=pages)

    @pl.when(c == 0)
    def _():
        _softmax_init(m_sc, l_sc, acc_sc)

    for p in range(pages):
        rows = slice(p * page, (p + 1) * page)
        for n in range(kv_heads):
            lanes = slice(n * hd2, (n + 1) * hd2)
            kbuf[rows, lanes] = k_pg[slot, p, pl.ds(n, page, stride=kv_heads), :].astype(BF16)
            vbuf[rows, lanes] = v_pg[slot, p, pl.ds(n, page, stride=kv_heads), :].astype(BF16)
    q = q_ref[...]
    s_sc[...] = _dot_nt(q, kbuf[...])

    @pl.when(c == last)
    def _():
        cols = slice((pages - 1) * page, pages * page)
        s_sc[:, cols] = s_sc[:, cols] + blast_ref[...]

    _softmax_step(s_sc[...], vbuf[...], m_sc, l_sc, acc_sc)

    @pl.when(c == last)
    def _():
        r = q.shape[0]
        t_row = lax.broadcasted_iota(jnp.int32, (r, n_new), 0) % n_new
        t_col = lax.broadcasted_iota(jnp.int32, (r, n_new), 1)
        _new_token_update(q.astype(F32), knew_ref[...], vnew_ref[...], bnew_ref[...], t_col <= t_row,
                          m_sc, l_sc, acc_sc)
        o = acc_sc[...] / jnp.sum(l_sc[...], axis=-1, keepdims=True)
        lam = _lambda_value(lam_ref, lambda_init)
        gr = group * n_new
        outs = []
        for n in range(kv_heads):
            o1 = o[(2 * n) * gr:(2 * n + 1) * gr, n * hd2:(n + 1) * hd2]
            o2 = o[(2 * n + 1) * gr:(2 * n + 2) * gr, n * hd2:(n + 1) * hd2]
            outs.append(o1 - lam * o2)
        on = jnp.concatenate(outs, axis=0)
        o_ref[...] = _rms(on, gsub_ref[...], SUBLN_EPS) * (1.0 - lambda_init)


def _diff_decode(page_table, q, knew, vnew, blast, bnew, lamp, gsub, cache_k, cache_v, *, pages, kv_heads,
                 group, n_new, lambda_init):
    db, r, nk = q.shape
    n_pages = page_table.shape[1]
    hd2 = nk // kv_heads
    page = cache_k.shape[1] // kv_heads
    nchunks = n_pages // pages
    pt = page_table.reshape(-1)

    per_b = lambda b, c, pt_ref: (b, 0, 0)
    const = lambda b, c, pt_ref: (0, 0)
    body = functools.partial(_diff_decode_body, pages=pages, page=page, kv_heads=kv_heads, group=group,
                             hd2=hd2, n_new=n_new, lambda_init=lambda_init)
    r_out = kv_heads * group * n_new
    grid_spec = pltpu.PrefetchScalarGridSpec(
        num_scalar_prefetch=1,
        grid=(db, nchunks),
        in_specs=[
            pl.BlockSpec((None, r, nk), per_b),
            pl.BlockSpec((None,) + knew.shape[1:], per_b),
            pl.BlockSpec((None,) + vnew.shape[1:], per_b),
            pl.BlockSpec(blast.shape, const),
            pl.BlockSpec(bnew.shape, const),
            pl.BlockSpec(lamp.shape, const),
            pl.BlockSpec(gsub.shape, const),
            pl.BlockSpec(memory_space=pl.ANY),
            pl.BlockSpec(memory_space=pl.ANY),
        ],
        out_specs=pl.BlockSpec((None, r_out, hd2), per_b),
        scratch_shapes=[
            pltpu.VMEM((2, pages, page * kv_heads, hd2), F32),
            pltpu.VMEM((2, pages, page * kv_heads, hd2), F32),
            pltpu.VMEM((pages * page, nk), BF16),
            pltpu.VMEM((pages * page, nk), BF16),
            pltpu.VMEM((r, pages * page), F32),
            pltpu.VMEM((r, LANES), F32),
            pltpu.VMEM((r, LANES), F32),
            pltpu.VMEM((r, nk), F32),
            pltpu.SemaphoreType.DMA((2,)),
            pltpu.SemaphoreType.DMA((2,)),
        ],
    )
    return pl.pallas_call(
        body,
        grid_spec=grid_spec,
        out_shape=jax.ShapeDtypeStruct((db, r_out, hd2), F32),
        compiler_params=_cparams("arbitrary", "arbitrary"),
        name="diff_decode",
    )(pt, q, knew, vnew, blast, bnew, lamp, gsub, cache_k, cache_v)


def _rope_tables(pos, half, heads):
    inv = ROPE_THETA ** (-jnp.arange(half, dtype=F32) / half)
    ang = pos.astype(F32)[:, None] * inv[None, :]
    cos, sin = jnp.cos(ang), jnp.sin(ang)
    cos2 = jnp.concatenate([cos, cos], axis=-1)
    sin2 = jnp.concatenate([-sin, sin], axis=-1)
    return jnp.tile(cos2, (1, heads)), jnp.tile(sin2, (1, heads))


def _bucket_of_distance(n):
    max_exact = REL_BUCKETS // 2
    nf = np.maximum(n, 1).astype(np.float32)
    ratio = np.log(nf / np.float32(max_exact)) / np.float32(math.log(REL_MAX_DIST / max_exact))
    large = max_exact + (ratio * np.float32(REL_BUCKETS - max_exact)).astype(np.int32)
    large = np.minimum(large, REL_BUCKETS - 1)
    return np.where(n < max_exact, n, large)


def _bias_by_distance(rel_bias, n, far_from, max_dist):
    far = _bucket_of_distance(np.arange(far_from, max_dist + 1))
    assert (far == far[0]).all(), "relative-position bias must be constant beyond the near tiles"
    select = np.zeros((n, REL_BUCKETS), np.float32)
    select[np.arange(n), _bucket_of_distance(np.arange(n))] = 1.0
    select[:, int(far[0])] -= 1.0
    picked = jnp.dot(jnp.asarray(select), rel_bias.astype(F32), precision=lax.Precision.HIGHEST)
    return LOG2E * picked.T


def _toeplitz(v, n, offset):
    heads = v.shape[0]
    length = 2 * n
    lo = n - 1 - offset
    u = jnp.pad(v, ((0, 0), (lo, 0)))[:, :length] if lo >= 0 else jnp.pad(v[:, -lo:], ((0, 0), (0, -lo)))
    x = jnp.tile(u, (1, n + 1))[:, :n * (length + 1)].reshape(heads, n, length + 1)[:, :, :n]
    return x[:, :, ::-1]


def kernel(x_prompt, x_sample, cache_mla_ckv, cache_mla_krope, cache_diff_k, cache_diff_v, page_table, g_mix, g_ffn, g_final, w_mla_down, g_mla_q, g_mla_kv, w_mla_uq, w_mla_uk, w_mla_uv, w_mla_o, w_diff_qkv, lam_q1, lam_k1, lam_q2, lam_k2, g_diff_sub, w_diff_o, rel_bias, w_ff_up, w_ff_down):
    nb, t, dm = x_prompt.shape
    db, n_new, _ = x_sample.shape
    n_pages = page_table.shape[1]
    page = cache_mla_ckv.shape[2]
    past = n_pages * page
    kv_lora, heads, nope = w_mla_uk.shape[1:]
    q_lora = g_mla_q.shape[1]
    rope = cache_mla_krope.shape[3]
    kv_heads, hd2 = cache_diff_k.shape[3:]
    nk = kv_heads * hd2
    dheads = rel_bias.shape[1]
    group = dheads // kv_heads
    assert g_mix.shape[0] == 2 and w_mla_down.shape[0] == 1 and w_diff_qkv.shape[0] == 1
    assert heads * rope == 4 * LANES and kv_lora % LANES == 0 and hd2 == LANES and rope * 2 == LANES

    tm_p = min(ROW_TILE, t)
    ns = db * n_new
    tm_s = min(ROW_TILE, ns)
    tq = min(ATTN_TILE, t)
    pages = min(PAGES_PER_STEP, n_pages)
    assert t % tm_p == 0 and ns % tm_s == 0 and t % tq == 0 and n_pages % pages == 0
    lambda_init = 0.8 - 0.6 * math.exp(-0.3 * 1)

    row = lambda v: v.reshape(1, -1).astype(F32)
    bf = lambda w: w.astype(BF16)

    wd = w_mla_down[0]
    r0 = q_lora + kv_lora
    half = rope // 2
    wd_ext = bf(jnp.concatenate([wd, wd[:, r0 + half:r0 + rope], wd[:, r0:r0 + half]], axis=1))
    wuq = w_mla_uq[0].reshape(q_lora, heads, nope + rope)
    wuq_ext = bf(jnp.concatenate([
        wuq[:, :, :nope].reshape(q_lora, heads * nope),
        wuq[:, :, nope:].reshape(q_lora, heads * rope),
        jnp.concatenate([wuq[:, :, nope + half:], wuq[:, :, nope:nope + half]], axis=-1).reshape(q_lora, heads * rope),
    ], axis=1))
    wuk_t = bf(jnp.transpose(w_mla_uk[0], (1, 2, 0)))
    wuv = bf(jnp.transpose(w_mla_uv[0], (1, 0, 2)))
    wo_mla = bf(w_mla_o[0])
    wqkv = bf(w_diff_qkv[0])
    wo_diff = bf(w_diff_o[0])
    wup = bf(w_ff_up)
    wdn = bf(w_ff_down)
    lamp = jnp.stack([lam_q1[0], lam_k1[0], lam_q2[0], lam_k2[0]]).astype(F32)
    gsub = row(g_diff_sub[0])
    mla_dims = (heads, q_lora, kv_lora, nope, rope)

    cos_p, sin_p = _rope_tables(jnp.arange(t, dtype=jnp.int32), half, heads)
    pos_s = past + (jnp.arange(ns, dtype=jnp.int32) % n_new)
    cos_s, sin_s = _rope_tables(pos_s, half, heads)
    bias_p = _bias_by_distance(rel_bias, 2 * tq, tq + 1, t + tq)
    d0 = _toeplitz(bias_p, tq, 0)
    d1 = _toeplitz(bias_p, tq, tq)

    hp = x_prompt.reshape(nb * t, dm)
    hs = x_sample.reshape(ns, dm)

    c_p, krt_p, kcat_p, q_p = _mla_pre(hp, row(g_mix[0]), wd_ext, row(g_mla_q[0]), row(g_mla_kv[0]), wuq_ext,
                                       wuk_t, cos_p, sin_p, nb=nb, tm=tm_p, dims=mla_dims)
    hp = _mla_attn_prompt(q_p, kcat_p, hp, wuv, wo_mla, nb=nb, t=t, tq=tq, kv_lora=kv_lora)
    hp = _ffn(hp, row(g_ffn[0]), wup[0], wdn[0], row(g_final), tm=tm_p, final_norm=False)

    c_s, krt_s, _, q_s = _mla_pre(hs, row(g_mix[0]), wd_ext, row(g_mla_q[0]), row(g_mla_kv[0]), wuq_ext,
                                  wuk_t, cos_s, sin_s, nb=1, tm=tm_s, dims=mla_dims)
    kw = kv_lora + LANES
    q_dec = q_s.reshape(heads, db, n_new, kw).transpose(1, 0, 2, 3).reshape(db, heads * n_new, kw)
    kr_s = krt_s[0].T
    knew = jnp.concatenate([c_s, kr_s, jnp.zeros((ns, LANES - rope), F32)], axis=-1).reshape(db, n_new, kw)
    knew = jnp.pad(knew, ((0, 0), (0, 8 - n_new), (0, 0)))
    o_lat = _mla_decode(page_table, q_dec, knew, cache_mla_ckv[0], jnp.swapaxes(cache_mla_krope[0], 1, 2),
                        pages=pages, n_new=n_new)
    o_lat = o_lat.reshape(db, heads, n_new, kv_lora).transpose(0, 2, 1, 3).reshape(ns, heads * kv_lora)
    hs = _mla_post(o_lat, hs, wuv, wo_mla, tm=tm_s)
    hs = _ffn(hs, row(g_ffn[0]), wup[0], wdn[0], row(g_final), tm=tm_s, final_norm=False)

    qd_p, k_p, v_p, kb_p, vb_p = _diff_pre(hp, row(g_mix[1]), wqkv, tm=tm_p, heads=dheads, hd2=hd2, nk=nk)
    hp = _diff_attn_prompt(qd_p, kb_p, vb_p, d0, d1, hp, lamp, gsub, wo_diff, nb=nb, t=t, tq=tq,
                           kv_heads=kv_heads, lambda_init=lambda_init)
    y_p = _ffn(hp, row(g_ffn[1]), wup[1], wdn[1], row(g_final), tm=tm_p, final_norm=True)

    qd_s, k_s, v_s, _, _ = _diff_pre(hs, row(g_mix[1]), wqkv, tm=tm_s, heads=dheads, hd2=hd2, nk=nk)
    q6 = qd_s.reshape(kv_heads, group, db, n_new, 2, hd2 // 2).transpose(2, 0, 4, 1, 3, 5)
    place = jnp.eye(2 * kv_heads, dtype=BF16).reshape(kv_heads, 2, 2 * kv_heads)
    q_dec = (q6[..., None, :] * place[None, :, :, None, None, :, None]).reshape(db, 2 * dheads * n_new, nk)
    pad_new = lambda a: jnp.pad(a.reshape(db, n_new, nk), ((0, 0), (0, 8 - n_new), (0, 0)))
    bias_s = _bias_by_distance(rel_bias, page + n_new, page + 1, past + n_new)
    b_last = jnp.stack([bias_s[:, tk + 1:tk + 1 + page][:, ::-1] for tk in range(n_new)], axis=1)
    b_new = jnp.stack([jnp.pad(bias_s[:, :tk + 1][:, ::-1], ((0, 0), (0, LANES - tk - 1))) for tk in range(n_new)],
                      axis=1)

    def per_row(b):
        b = jnp.broadcast_to(b.reshape(kv_heads, 1, group, n_new, -1), (kv_heads, 2, group, n_new, b.shape[-1]))
        return b.reshape(2 * dheads * n_new, -1)

    b_last, b_new = per_row(b_last), per_row(b_new)
    o_d = _diff_decode(page_table, q_dec, pad_new(k_s), pad_new(v_s), b_last, b_new, lamp, gsub,
                       cache_diff_k[0].reshape(-1, page * kv_heads, hd2),
                       cache_diff_v[0].reshape(-1, page * kv_heads, hd2),
                       pages=pages, kv_heads=kv_heads, group=group, n_new=n_new, lambda_init=lambda_init)
    o_d = o_d.reshape(db, kv_heads, group, n_new, hd2).transpose(0, 3, 1, 2, 4).reshape(ns, dheads * hd2)
    hs = _proj_resid(o_d, hs, wo_diff, tm=tm_s)
    y_s = _ffn(hs, row(g_ffn[1]), wup[1], wdn[1], row(g_final), tm=tm_s, final_norm=True)

    return (y_p.reshape(nb, t, dm), y_s.reshape(db, n_new, dm),
            c_p.reshape(1, nb, t, kv_lora), jnp.swapaxes(krt_p, 1, 2)[None],
            k_p.reshape(1, nb, t, kv_heads, hd2), v_p.reshape(1, nb, t, kv_heads, hd2),
            c_s.reshape(1, db, n_new, kv_lora), kr_s.reshape(1, db, n_new, rope),
            k_s.reshape(1, db, n_new, kv_heads, hd2), v_s.reshape(1, db, n_new, kv_heads, hd2))
```

```python
import functools
import math

import numpy as np
import jax
import jax.numpy as jnp
from jax import lax
from jax.experimental import pallas as pl
from jax.experimental.pallas import tpu as pltpu

F32 = jnp.float32
BF16 = jnp.bfloat16

ROPE_THETA = 10000.0
REL_BUCKETS = 32
REL_MAX_DIST = 128
NORM_EPS = 1e-6
SUBLN_EPS = 1e-5
NEG = -1e30
LOG2E = math.log2(math.e)

LANES = 128
ROW_TILE = 512
ATTN_TILE = 256
PAGES_PER_STEP = 32
VMEM_LIMIT = 56 * 1024 * 1024


def _cparams(*sem):
    return pltpu.CompilerParams(dimension_semantics=sem, vmem_limit_bytes=VMEM_LIMIT)


def _rms(x, g, eps):
    return x * lax.rsqrt(jnp.mean(x * x, axis=-1, keepdims=True) + eps) * g


def _dot(a, b):
    return jnp.dot(a, b, preferred_element_type=F32)


def _dot_nt(a, b):
    return lax.dot_general(a, b, (((1,), (1,)), ((), ())), preferred_element_type=F32)


def _lane_block_sum(p):
    out = p[:, :LANES]
    for j in range(1, p.shape[1] // LANES):
        out = out + p[:, j * LANES:(j + 1) * LANES]
    return out


def _softmax_step(s, v, m_sc, l_sc, acc_sc):
    m_prev = m_sc[...]
    m_new = jnp.maximum(m_prev, jnp.max(s, axis=-1, keepdims=True))
    alpha = jnp.exp2(m_prev - m_new)
    p = jnp.exp2(s - jnp.tile(m_new, (1, s.shape[1] // LANES)))
    l_sc[...] = alpha * l_sc[...] + _lane_block_sum(p)
    acc_sc[...] = jnp.tile(alpha, (1, acc_sc.shape[1] // LANES)) * acc_sc[...] + _dot(p.astype(BF16), v)
    m_sc[...] = m_new


def _softmax_init(m_sc, l_sc, acc_sc):
    m_sc[...] = jnp.full(m_sc.shape, NEG, F32)
    l_sc[...] = jnp.zeros(l_sc.shape, F32)
    acc_sc[...] = jnp.zeros(acc_sc.shape, F32)


def _mla_pre_body(h_ref, g_ref, wd_ref, gq_ref, gkv_ref, wuq_ref, wuk_ref, cos_ref, sin_ref,
                  c_ref, kr_ref, kcat_ref, q_ref, *, scale, heads, q_lora, kv_lora, nope, rope):
    tm = h_ref.shape[0]
    a = _rms(h_ref[...], g_ref[...], NORM_EPS).astype(BF16)
    d = _dot(a, wd_ref[...])
    cq = _rms(d[:, :q_lora], gq_ref[...], NORM_EPS)
    c = _rms(d[:, q_lora:q_lora + kv_lora], gkv_ref[...], NORM_EPS)
    cosq = cos_ref[...]
    sinq = sin_ref[...]
    r0 = q_lora + kv_lora
    kr = d[:, r0:r0 + rope] * cosq[:, :rope] + d[:, r0 + rope:r0 + 2 * rope] * sinq[:, :rope]
    c_ref[...] = c
    pad = jnp.zeros((tm, LANES - rope), F32)
    kr_ref[...] = jnp.concatenate([kr, pad], axis=-1).T[:rope]
    kcat_ref[...] = jnp.concatenate([c, kr, pad], axis=-1).astype(BF16)
    q = _dot(cq.astype(BF16), wuq_ref[...])
    hn = heads * nope
    hr = heads * rope
    qr = (q[:, hn:hn + hr] * cosq + q[:, hn + hr:hn + 2 * hr] * sinq) * scale
    for h in range(heads):
        ql = _dot(q[:, h * nope:(h + 1) * nope].astype(BF16), wuk_ref[h]) * scale
        q_ref[h] = jnp.concatenate([ql, qr[:, h * rope:(h + 1) * rope], pad], axis=-1).astype(BF16)


def _mla_pre(h, g, wd, gq, gkv, wuq, wuk, cosq, sinq, *, nb, tm, dims):
    n, dm = h.shape
    t = cosq.shape[0]
    nt = t // tm
    heads, q_lora, kv_lora, nope, rope = dims
    kw = kv_lora + LANES
    row = lambda ti, b: (b * nt + ti, 0)
    const2 = lambda ti, b: (0, 0)
    body = functools.partial(_mla_pre_body, scale=LOG2E * (nope + rope) ** -0.5, heads=heads, q_lora=q_lora,
                             kv_lora=kv_lora, nope=nope, rope=rope)
    return pl.pallas_call(
        body,
        grid=(nt, nb),
        in_specs=[
            pl.BlockSpec((tm, dm), row),
            pl.BlockSpec(g.shape, const2),
            pl.BlockSpec(wd.shape, const2),
            pl.BlockSpec(gq.shape, const2),
            pl.BlockSpec(gkv.shape, const2),
            pl.BlockSpec(wuq.shape, const2),
            pl.BlockSpec(wuk.shape, lambda ti, b: (0, 0, 0)),
            pl.BlockSpec((tm, cosq.shape[1]), lambda ti, b: (ti, 0)),
            pl.BlockSpec((tm, sinq.shape[1]), lambda ti, b: (ti, 0)),
        ],
        out_specs=[
            pl.BlockSpec((tm, kv_lora), row),
            pl.BlockSpec((None, rope, tm), lambda ti, b: (b, 0, ti)),
            pl.BlockSpec((tm, kw), row),
            pl.BlockSpec((heads, tm, kw), lambda ti, b: (0, b * nt + ti, 0)),
        ],
        out_shape=[
            jax.ShapeDtypeStruct((n, kv_lora), F32),
            jax.ShapeDtypeStruct((nb, rope, t), F32),
            jax.ShapeDtypeStruct((n, kw), BF16),
            jax.ShapeDtypeStruct((heads, n, kw), BF16),
        ],
        compiler_params=_cparams("arbitrary", "arbitrary"),
        name="mla_pre",
    )(h, g, wd, gq, gkv, wuq, wuk, cosq, sinq)


def _mla_out(o_heads, wuv_ref, wo_ref, resid):
    ov = [_dot(o.astype(BF16), wuv_ref[h]) for h, o in enumerate(o_heads)]
    o = jnp.concatenate(ov, axis=-1).astype(BF16)
    return resid + _dot(o, wo_ref[...])


def _mla_attn_body(q_ref, k_ref, h_ref, wuv_ref, wo_ref, o_ref, m_sc, l_sc, acc_sc, *, heads, kv_lora, tq):
    i = pl.program_id(1)
    q = q_ref[...].reshape(heads * tq, q_ref.shape[2])
    _softmax_init(m_sc, l_sc, acc_sc)

    def keys(j):
        return k_ref[pl.ds(pl.multiple_of(j * tq, tq), tq), :]

    def full_step(j, s):
        s_next = _dot_nt(q, keys(j + 1))
        _softmax_step(s, keys(j)[:, :kv_lora], m_sc, l_sc, acc_sc)
        return s_next

    s = lax.fori_loop(0, i, full_step, _dot_nt(q, keys(0)))
    row = lax.broadcasted_iota(jnp.int32, (tq, tq), 0)
    col = lax.broadcasted_iota(jnp.int32, (tq, tq), 1)
    s = jnp.where((col <= row)[None], s.reshape(heads, tq, tq), NEG).reshape(heads * tq, tq)
    _softmax_step(s, keys(i)[:, :kv_lora], m_sc, l_sc, acc_sc)

    o = acc_sc[...] / jnp.sum(l_sc[...], axis=-1, keepdims=True)
    o_heads = [o[h * tq:(h + 1) * tq] for h in range(heads)]
    o_ref[...] = _mla_out(o_heads, wuv_ref, wo_ref, h_ref[...])


def _mla_attn_prompt(q, kcat, h, wuv, wo, *, nb, t, tq, kv_lora):
    heads, n, kw = q.shape
    dm = h.shape[1]
    nq = t // tq
    body = functools.partial(_mla_attn_body, heads=heads, kv_lora=kv_lora, tq=tq)
    return pl.pallas_call(
        body,
        grid=(nb, nq),
        in_specs=[
            pl.BlockSpec((heads, tq, kw), lambda b, i: (0, b * nq + i, 0)),
            pl.BlockSpec((t, kw), lambda b, i: (b, 0)),
            pl.BlockSpec((tq, dm), lambda b, i: (b * nq + i, 0)),
            pl.BlockSpec(wuv.shape, lambda b, i: (0, 0, 0)),
            pl.BlockSpec(wo.shape, lambda b, i: (0, 0)),
        ],
        out_specs=pl.BlockSpec((tq, dm), lambda b, i: (b * nq + i, 0)),
        out_shape=jax.ShapeDtypeStruct((n, dm), F32),
        scratch_shapes=[
            pltpu.VMEM((heads * tq, LANES), F32),
            pltpu.VMEM((heads * tq, LANES), F32),
            pltpu.VMEM((heads * tq, kv_lora), F32),
        ],
        compiler_params=_cparams("arbitrary", "arbitrary"),
        name="mla_attn_prompt",
    )(q, kcat, h, wuv, wo)


def _mla_post_body(o_ref, h_ref, wuv_ref, wo_ref, out_ref, *, heads, kv_lora):
    o = o_ref[...]
    o_heads = [o[:, h * kv_lora:(h + 1) * kv_lora] for h in range(heads)]
    out_ref[...] = _mla_out(o_heads, wuv_ref, wo_ref, h_ref[...])


def _mla_post(o_lat, h, wuv, wo, *, tm):
    n, dm = h.shape
    heads, kv_lora, _ = wuv.shape
    body = functools.partial(_mla_post_body, heads=heads, kv_lora=kv_lora)
    return pl.pallas_call(
        body,
        grid=(n // tm,),
        in_specs=[
            pl.BlockSpec((tm, o_lat.shape[1]), lambda i: (i, 0)),
            pl.BlockSpec((tm, dm), lambda i: (i, 0)),
            pl.BlockSpec(wuv.shape, lambda i: (0, 0, 0)),
            pl.BlockSpec(wo.shape, lambda i: (0, 0)),
        ],
        out_specs=pl.BlockSpec((tm, dm), lambda i: (i, 0)),
        out_shape=jax.ShapeDtypeStruct((n, dm), F32),
        compiler_params=_cparams("arbitrary"),
        name="mla_post",
    )(o_lat, h, wuv, wo)


def _ffn_body(h_ref, g_ref, wup_ref, wdn_ref, gf_ref, o_ref, *, chunk, final_norm):
    x = h_ref[...]
    a = _rms(x, g_ref[...], NORM_EPS).astype(BF16)
    acc = x
    for j in range(wup_ref.shape[1] // chunk):
        u = jnp.maximum(_dot(a, wup_ref[:, j * chunk:(j + 1) * chunk]), 0.0)
        acc = acc + _dot((u * u).astype(BF16), wdn_ref[j * chunk:(j + 1) * chunk, :])
    if final_norm:
        acc = _rms(acc, gf_ref[...], NORM_EPS)
    o_ref[...] = acc


def _ffn(h, g, wup, wdn, gf, *, tm, final_norm):
    n, dm = h.shape
    body = functools.partial(_ffn_body, chunk=min(1024, wup.shape[1]), final_norm=final_norm)
    const = lambda i: (0, 0)
    return pl.pallas_call(
        body,
        grid=(n // tm,),
        in_specs=[
            pl.BlockSpec((tm, dm), lambda i: (i, 0)),
            pl.BlockSpec(g.shape, const),
            pl.BlockSpec(wup.shape, const, pipeline_mode=pl.Buffered(1)),
            pl.BlockSpec(wdn.shape, const, pipeline_mode=pl.Buffered(1)),
            pl.BlockSpec(gf.shape, const),
        ],
        out_specs=pl.BlockSpec((tm, dm), lambda i: (i, 0)),
        out_shape=jax.ShapeDtypeStruct((n, dm), F32),
        compiler_params=_cparams("arbitrary"),
        name="ffn_final" if final_norm else "ffn",
    )(h, g, wup, wdn, gf)


def _diff_pre_body(h_ref, g_ref, w_ref, q_ref, k_ref, v_ref, kb_ref, vb_ref, *, scale, heads, hd2, nk):
    a = _rms(h_ref[...], g_ref[...], NORM_EPS).astype(BF16)
    proj = _dot(a, w_ref[...])
    nq = heads * hd2
    for h in range(heads):
        q_ref[h] = (proj[:, h * hd2:(h + 1) * hd2] * scale).astype(BF16)
    k = proj[:, nq:nq + nk]
    v = proj[:, nq + nk:nq + 2 * nk]
    kv_heads = nk // hd2
    tm = h_ref.shape[0]
    for n in range(kv_heads):
        k_ref[pl.ds(n, tm, stride=kv_heads), :] = k[:, n * hd2:(n + 1) * hd2]
        v_ref[pl.ds(n, tm, stride=kv_heads), :] = v[:, n * hd2:(n + 1) * hd2]
    kb_ref[...] = k.astype(BF16)
    vb_ref[...] = v.astype(BF16)


def _diff_pre(h, g, w, *, tm, heads, hd2, nk):
    n, dm = h.shape
    body = functools.partial(_diff_pre_body, scale=LOG2E * (hd2 // 2) ** -0.5, heads=heads, hd2=hd2, nk=nk)
    row = lambda i: (i, 0)
    return pl.pallas_call(
        body,
        grid=(n // tm,),
        in_specs=[
            pl.BlockSpec((tm, dm), row),
            pl.BlockSpec(g.shape, lambda i: (0, 0)),
            pl.BlockSpec(w.shape, lambda i: (0, 0)),
        ],
        out_specs=[
            pl.BlockSpec((heads, tm, hd2), lambda i: (0, i, 0)),
            pl.BlockSpec((tm * nk // hd2, hd2), row),
            pl.BlockSpec((tm * nk // hd2, hd2), row),
            pl.BlockSpec((tm, nk), row),
            pl.BlockSpec((tm, nk), row),
        ],
        out_shape=[
            jax.ShapeDtypeStruct((heads, n, hd2), BF16),
            jax.ShapeDtypeStruct((n * nk // hd2, hd2), F32),
            jax.ShapeDtypeStruct((n * nk // hd2, hd2), F32),
            jax.ShapeDtypeStruct((n, nk), BF16),
            jax.ShapeDtypeStruct((n, nk), BF16),
        ],
        compiler_params=_cparams("arbitrary"),
        name="diff_pre",
    )(h, g, w)


def _lambda_value(lam_ref, lambda_init):
    lp = lam_ref[...]
    e1 = jnp.exp(jnp.sum(lp[0:1] * lp[1:2], axis=-1, keepdims=True))
    e2 = jnp.exp(jnp.sum(lp[2:3] * lp[3:4], axis=-1, keepdims=True))
    return e1 - e2 + lambda_init


def _diff_attn_body(q_ref, k_ref, v_ref, d0_ref, d1_ref, h_ref, lam_ref, gsub_ref, wo_ref, o_ref,
                    m_sc, l_sc, acc_sc, *, kv_heads, group, hd2, tq, lambda_init):
    i = pl.program_id(1)
    lam = _lambda_value(lam_ref, lambda_init)
    rows = group * tq
    first_half = lax.broadcasted_iota(jnp.int32, (rows, hd2), 1) < hd2 // 2
    row = lax.broadcasted_iota(jnp.int32, (tq, tq), 0)
    col = lax.broadcasted_iota(jnp.int32, (tq, tq), 1)
    causal = (col <= row)[None, None]
    qzs = []
    for n in range(kv_heads):
        qn = q_ref[n * group:(n + 1) * group].reshape(rows, hd2)
        zero = jnp.zeros_like(qn)
        qzs.append(jnp.concatenate([jnp.where(first_half, qn, zero), jnp.where(first_half, zero, qn)], axis=0))
    _softmax_init(m_sc, l_sc, acc_sc)

    def step(j, bias_ref, masked):
        start = pl.multiple_of(j * tq, tq)
        for n in range(kv_heads):
            lanes = slice(n * hd2, (n + 1) * hd2)
            s = _dot_nt(qzs[n], k_ref[pl.ds(start, tq), lanes])
            if bias_ref is not None:
                s = s.reshape(2, group, tq, tq) + bias_ref[n * group:(n + 1) * group][None]
                if masked:
                    s = jnp.where(causal, s, NEG)
                s = s.reshape(2 * rows, tq)
            _softmax_step(s, v_ref[pl.ds(start, tq), lanes], m_sc.at[n], l_sc.at[n], acc_sc.at[n])

    def far_step(j, carry):
        step(j, None, False)
        return carry

    lax.fori_loop(0, jnp.maximum(i - 1, 0), far_step, 0)

    @pl.when(i >= 1)
    def _():
        step(i - 1, d1_ref, False)

    step(i, d0_ref, True)

    outs = []
    for n in range(kv_heads):
        o = acc_sc[n] / jnp.sum(l_sc[n], axis=-1, keepdims=True)
        o = o[:rows] - lam * o[rows:]
        o = _rms(o, gsub_ref[...], SUBLN_EPS) * (1.0 - lambda_init)
        outs.extend(o[g * tq:(g + 1) * tq] for g in range(group))
    attn = jnp.concatenate(outs, axis=-1).astype(BF16)
    o_ref[...] = h_ref[...] + _dot(attn, wo_ref[...])


def _diff_attn_prompt(q, kb, vb, d0, d1, h, lamp, gsub, wo, *, nb, t, tq, kv_heads, lambda_init):
    heads, n, hd2 = q.shape
    dm = h.shape[1]
    nq = t // tq
    group = heads // kv_heads
    nk = kb.shape[1]
    body = functools.partial(_diff_attn_body, kv_heads=kv_heads, group=group, hd2=hd2, tq=tq,
                             lambda_init=lambda_init)
    c2 = lambda b, i: (0, 0)
    c3 = lambda b, i: (0, 0, 0)
    return pl.pallas_call(
        body,
        grid=(nb, nq),
        in_specs=[
            pl.BlockSpec((heads, tq, hd2), lambda b, i: (0, b * nq + i, 0)),
            pl.BlockSpec((t, nk), lambda b, i: (b, 0)),
            pl.BlockSpec((t, nk), lambda b, i: (b, 0)),
            pl.BlockSpec(d0.shape, c3),
            pl.BlockSpec(d1.shape, c3),
            pl.BlockSpec((tq, dm), lambda b, i: (b * nq + i, 0)),
            pl.BlockSpec(lamp.shape, c2),
            pl.BlockSpec(gsub.shape, c2),
            pl.BlockSpec(wo.shape, c2),
        ],
        out_specs=pl.BlockSpec((tq, dm), lambda b, i: (b * nq + i, 0)),
        out_shape=jax.ShapeDtypeStruct((n, dm), F32),
        scratch_shapes=[
            pltpu.VMEM((kv_heads, 2 * group * tq, LANES), F32),
            pltpu.VMEM((kv_heads, 2 * group * tq, LANES), F32),
            pltpu.VMEM((kv_heads, 2 * group * tq, hd2), F32),
        ],
        compiler_params=_cparams("arbitrary", "arbitrary"),
        name="diff_attn_prompt",
    )(q, kb, vb, d0, d1, h, lamp, gsub, wo)


def _proj_resid_body(x_ref, h_ref, w_ref, o_ref):
    o_ref[...] = h_ref[...] + _dot(x_ref[...].astype(BF16), w_ref[...])


def _proj_resid(x, h, w, *, tm):
    n, dm = h.shape
    return pl.pallas_call(
        _proj_resid_body,
        grid=(n // tm,),
        in_specs=[
            pl.BlockSpec((tm, x.shape[1]), lambda i: (i, 0)),
            pl.BlockSpec((tm, dm), lambda i: (i, 0)),
            pl.BlockSpec(w.shape, lambda i: (0, 0)),
        ],
        out_specs=pl.BlockSpec((tm, dm), lambda i: (i, 0)),
        out_shape=jax.ShapeDtypeStruct((n, dm), F32),
        compiler_params=_cparams("arbitrary"),
        name="proj_resid",
    )(x, h, w)


def _new_token_update(qf, k_new, v_new, bias_new, valid, m_sc, l_sc, acc_sc):
    n_new = valid.shape[1]
    lane0 = lax.broadcasted_iota(jnp.int32, (qf.shape[0], LANES), 1) == 0
    s_cols = []
    for t in range(n_new):
        s_t = jnp.sum(qf * k_new[t:t + 1, :], axis=-1, keepdims=True)
        if bias_new is not None:
            s_t = s_t + bias_new[:, t:t + 1]
        s_cols.append(jnp.where(valid[:, t:t + 1], s_t, NEG))
    m_prev = m_sc[...]
    m_new = m_prev
    for s_t in s_cols:
        m_new = jnp.maximum(m_new, s_t)
    alpha = jnp.exp2(m_prev - m_new)
    l = alpha * l_sc[...]
    acc = jnp.tile(alpha, (1, acc_sc.shape[1] // LANES)) * acc_sc[...]
    for t, s_t in enumerate(s_cols):
        p_t = jnp.exp2(s_t - m_new)
        l = l + jnp.where(lane0, p_t, 0.0)
        acc = acc + jnp.tile(p_t, (1, acc_sc.shape[1] // LANES)) * v_new[t:t + 1, :]
    m_sc[...] = m_new
    l_sc[...] = l
    acc_sc[...] = acc


def _page_ring(pt_ref, srcs, bufs, sems, *, pages):
    nchunks = pl.num_programs(1)
    step = pl.program_id(0) * nchunks + pl.program_id(1)
    total = pl.num_programs(0) * nchunks
    slot = lax.rem(step, 2)

    def copies(step_idx, slot_idx):
        out = []
        for p in range(pages):
            pg = pt_ref[step_idx * pages + p]
            for src, buf, sem in zip(srcs, bufs, sems):
                out.append(pltpu.make_async_copy(src.at[pg], buf.at[slot_idx, p], sem.at[slot_idx]))
        return out

    def start_all(step_idx, slot_idx):
        for n, cp in enumerate(copies(step_idx, slot_idx)):
            cp.start(priority=(n // len(srcs)) % 2)

    @pl.when(step == 0)
    def _():
        start_all(step, slot)

    @pl.when(step + 1 < total)
    def _():
        start_all(step + 1, 1 - slot)

    for cp in copies(step, slot):
        cp.wait()
    return slot


def _mla_decode_body(pt_ref, q_ref, knew_ref, ckv_hbm, kr_hbm, o_ref, ckv_pg, kr_pg, cbuf, krbuf, m_sc, l_sc, acc_sc,
                     sem_c, sem_r, *, pages, page, kv_lora, rope, n_new):
    c = pl.program_id(1)
    slot = _page_ring(pt_ref, (ckv_hbm, kr_hbm), (ckv_pg, kr_pg), (sem_c, sem_r), pages=pages)

    @pl.when(c == 0)
    def _():
        _softmax_init(m_sc, l_sc, acc_sc)

    krbuf[rope:, :] = jnp.zeros((krbuf.shape[0] - rope, krbuf.shape[1]), BF16)
    for p in range(pages):
        cbuf[p * page:(p + 1) * page, :] = ckv_pg[slot, p].astype(BF16)
        krbuf[:rope, p * page:(p + 1) * page] = kr_pg[slot, p].astype(BF16)
    q = q_ref[...]
    ck = cbuf[...]
    s = _dot_nt(q[:, :kv_lora], ck) + _dot(q[:, kv_lora:], krbuf[...])
    _softmax_step(s, ck, m_sc, l_sc, acc_sc)

    @pl.when(c == pl.num_programs(1) - 1)
    def _():
        r = q.shape[0]
        k_new = knew_ref[...]
        t_row = lax.broadcasted_iota(jnp.int32, (r, n_new), 0) % n_new
        t_col = lax.broadcasted_iota(jnp.int32, (r, n_new), 1)
        _new_token_update(q.astype(F32), k_new, k_new[:, :kv_lora], None, t_col <= t_row, m_sc, l_sc, acc_sc)
        o_ref[...] = acc_sc[...] / jnp.sum(l_sc[...], axis=-1, keepdims=True)


def _mla_decode(page_table, q, knew, cache_ckv, cache_kr, *, pages, n_new):
    db, r, kw = q.shape
    n_pages = page_table.shape[1]
    page, kv_lora = cache_ckv.shape[1:]
    rope = cache_kr.shape[1]
    nchunks = n_pages // pages
    pt = page_table.reshape(-1)

    body = functools.partial(_mla_decode_body, pages=pages, page=page, kv_lora=kv_lora, rope=rope, n_new=n_new)
    grid_spec = pltpu.PrefetchScalarGridSpec(
        num_scalar_prefetch=1,
        grid=(db, nchunks),
        in_specs=[
            pl.BlockSpec((None, r, kw), lambda b, c, pt_ref: (b, 0, 0)),
            pl.BlockSpec((None,) + knew.shape[1:], lambda b, c, pt_ref: (b, 0, 0)),
            pl.BlockSpec(memory_space=pl.ANY),
            pl.BlockSpec(memory_space=pl.ANY),
        ],
        out_specs=pl.BlockSpec((None, r, kv_lora), lambda b, c, pt_ref: (b, 0, 0)),
        scratch_shapes=[
            pltpu.VMEM((2, pages, page, kv_lora), F32),
            pltpu.VMEM((2, pages, rope, page), F32),
            pltpu.VMEM((pages * page, kv_lora), BF16),
            pltpu.VMEM((kw - kv_lora, pages * page), BF16),
            pltpu.VMEM((r, LANES), F32),
            pltpu.VMEM((r, LANES), F32),
            pltpu.VMEM((r, kv_lora), F32),
            pltpu.SemaphoreType.DMA((2,)),
            pltpu.SemaphoreType.DMA((2,)),
        ],
    )
    return pl.pallas_call(
        body,
        grid_spec=grid_spec,
        out_shape=jax.ShapeDtypeStruct((db, r, kv_lora), F32),
        compiler_params=_cparams("arbitrary", "arbitrary"),
        name="mla_decode",
    )(pt, q, knew, cache_ckv, cache_kr)


def _diff_decode_body(pt_ref, q_ref, knew_ref, vnew_ref, blast_ref, bnew_ref, lam_ref, gsub_ref, k_hbm, v_hbm,
                      o_ref, k_pg, v_pg, kbuf, vbuf, s_sc, m_sc, l_sc, acc_sc, sem_k, sem_v, *,
                      pages, page, kv_heads, group, hd2, n_new, lambda_init):
    c = pl.program_id(1)
    last = pl.num_programs(1) - 1
    slot = _page_ring(pt_ref, (k_hbm, v_hbm), (k_pg, v_pg), (sem_k, sem_v), pages=pages)

    @pl.when(c == 0)
    def _():
        _softmax_init(m_sc, l_sc, acc_sc)

    for p in range(pages):
        rows = slice(p * page, (p + 1) * page)
        for n in range(kv_heads):
            lanes = slice(n * hd2, (n + 1) * hd2)
            kbuf[rows, lanes] = k_pg[slot, p, pl.ds(n, page, stride=kv_heads), :].astype(BF16)
            vbuf[rows, lanes] = v_pg[slot, p, pl.ds(n, page, stride=kv_heads), :].astype(BF16)
    q = q_ref[...]
    s_sc[...] = _dot_nt(q, kbuf[...])

    @pl.when(c == last)
    def _():
        cols = slice((pages - 1) * page, pages * page)
        s_sc[:, cols] = s_sc[:, cols] + blast_ref[...]

    _softmax_step(s_sc[...], vbuf[...], m_sc, l_sc, acc_sc)

    @pl.when(c == last)
    def _():
        r = q.shape[0]
        t_row = lax.broadcasted_iota(jnp.int32, (r, n_new), 0) % n_new
        t_col = lax.broadcasted_iota(jnp.int32, (r, n_new), 1)
        _new_token_update(q.astype(F32), knew_ref[...], vnew_ref[...], bnew_ref[...], t_col <= t_row,
                          m_sc, l_sc, acc_sc)
        o = acc_sc[...] / jnp.sum(l_sc[...], axis=-1, keepdims=True)
        lam = _lambda_value(lam_ref, lambda_init)
        gr = group * n_new
        outs = []
        for n in range(kv_heads):
            o1 = o[(2 * n) * gr:(2 * n + 1) * gr, n * hd2:(n + 1) * hd2]
            o2 = o[(2 * n + 1) * gr:(2 * n + 2) * gr, n * hd2:(n + 1) * hd2]
            outs.append(o1 - lam * o2)
        on = jnp.concatenate(outs, axis=0)
        o_ref[...] = _rms(on, gsub_ref[...], SUBLN_EPS) * (1.0 - lambda_init)


def _diff_decode(page_table, q, knew, vnew, blast, bnew, lamp, gsub, cache_k, cache_v, *, pages, kv_heads,
                 group, n_new, lambda_init):
    db, r, nk = q.shape
    n_pages = page_table.shape[1]
    hd2 = nk // kv_heads
    page = cache_k.shape[1] // kv_heads
    nchunks = n_pages // pages
    pt = page_table.reshape(-1)

    per_b = lambda b, c, pt_ref: (b, 0, 0)
    const = lambda b, c, pt_ref: (0, 0)
    body = functools.partial(_diff_decode_body, pages=pages, page=page, kv_heads=kv_heads, group=group,
                             hd2=hd2, n_new=n_new, lambda_init=lambda_init)
    r_out = kv_heads * group * n_new
    grid_spec = pltpu.PrefetchScalarGridSpec(
        num_scalar_prefetch=1,
        grid=(db, nchunks),
        in_specs=[
            pl.BlockSpec((None, r, nk), per_b),
            pl.BlockSpec((None,) + knew.shape[1:], per_b),
            pl.BlockSpec((None,) + vnew.shape[1:], per_b),
            pl.BlockSpec(blast.shape, const),
            pl.BlockSpec(bnew.shape, const),
            pl.BlockSpec(lamp.shape, const),
            pl.BlockSpec(gsub.shape, const),
            pl.BlockSpec(memory_space=pl.ANY),
            pl.BlockSpec(memory_space=pl.ANY),
        ],
        out_specs=pl.BlockSpec((None, r_out, hd2), per_b),
        scratch_shapes=[
            pltpu.VMEM((2, pages, page * kv_heads, hd2), F32),
            pltpu.VMEM((2, pages, page * kv_heads, hd2), F32),
            pltpu.VMEM((pages * page, nk), BF16),
            pltpu.VMEM((pages * page, nk), BF16),
            pltpu.VMEM((r, pages * page), F32),
            pltpu.VMEM((r, LANES), F32),
            pltpu.VMEM((r, LANES), F32),
            pltpu.VMEM((r, nk), F32),
            pltpu.SemaphoreType.DMA((2,)),
            pltpu.SemaphoreType.DMA((2,)),
        ],
    )
    return pl.pallas_call(
        body,
        grid_spec=grid_spec,
        out_shape=jax.ShapeDtypeStruct((db, r_out, hd2), F32),
        compiler_params=_cparams("arbitrary", "arbitrary"),
        name="diff_decode",
    )(pt, q, knew, vnew, blast, bnew, lamp, gsub, cache_k, cache_v)


def _rope_tables(pos, half, heads):
    inv = ROPE_THETA ** (-jnp.arange(half, dtype=F32) / half)
    ang = pos.astype(F32)[:, None] * inv[None, :]
    cos, sin = jnp.cos(ang), jnp.sin(ang)
    cos2 = jnp.concatenate([cos, cos], axis=-1)
    sin2 = jnp.concatenate([-sin, sin], axis=-1)
    return jnp.tile(cos2, (1, heads)), jnp.tile(sin2, (1, heads))


def _bucket_of_distance(n):
    max_exact = REL_BUCKETS // 2
    nf = np.maximum(n, 1).astype(np.float32)
    ratio = np.log(nf / np.float32(max_exact)) / np.float32(math.log(REL_MAX_DIST / max_exact))
    large = max_exact + (ratio * np.float32(REL_BUCKETS - max_exact)).astype(np.int32)
    large = np.minimum(large, REL_BUCKETS - 1)
    return np.where(n < max_exact, n, large)


def _bias_by_distance(rel_bias, n, far_from, max_dist):
    far = _bucket_of_distance(np.arange(far_from, max_dist + 1))
    assert (far == far[0]).all(), "relative-position bias must be constant beyond the near tiles"
    select = np.zeros((n, REL_BUCKETS), np.float32)
    select[np.arange(n), _bucket_of_distance(np.arange(n))] = 1.0
    select[:, int(far[0])] -= 1.0
    picked = jnp.dot(jnp.asarray(select), rel_bias.astype(F32), precision=lax.Precision.HIGHEST)
    return LOG2E * picked.T


def _toeplitz(v, n, offset):
    heads = v.shape[0]
    length = 2 * n
    lo = n - 1 - offset
    u = jnp.pad(v, ((0, 0), (lo, 0)))[:, :length] if lo >= 0 else jnp.pad(v[:, -lo:], ((0, 0), (0, -lo)))
    x = jnp.tile(u, (1, n + 1))[:, :n * (length + 1)].reshape(heads, n, length + 1)[:, :, :n]
    return x[:, :, ::-1]


def kernel(x_prompt, x_sample, cache_mla_ckv, cache_mla_krope, cache_diff_k, cache_diff_v, page_table, g_mix, g_ffn, g_final, w_mla_down, g_mla_q, g_mla_kv, w_mla_uq, w_mla_uk, w_mla_uv, w_mla_o, w_diff_qkv, lam_q1, lam_k1, lam_q2, lam_k2, g_diff_sub, w_diff_o, rel_bias, w_ff_up, w_ff_down):
    nb, t, dm = x_prompt.shape
    db, n_new, _ = x_sample.shape
    n_pages = page_table.shape[1]
    page = cache_mla_ckv.shape[2]
    past = n_pages * page
    kv_lora, heads, nope = w_mla_uk.shape[1:]
    q_lora = g_mla_q.shape[1]
    rope = cache_mla_krope.shape[3]
    kv_heads, hd2 = cache_diff_k.shape[3:]
    nk = kv_heads * hd2
    dheads = rel_bias.shape[1]
    group = dheads // kv_heads
    assert g_mix.shape[0] == 2 and w_mla_down.shape[0] == 1 and w_diff_qkv.shape[0] == 1
    assert heads * rope == 4 * LANES and kv_lora % LANES == 0 and hd2 == LANES and rope * 2 == LANES

    tm_p = min(ROW_TILE, t)
    ns = db * n_new
    tm_s = min(ROW_TILE, ns)
    tq = min(ATTN_TILE, t)
    pages = min(PAGES_PER_STEP, n_pages)
    assert t % tm_p == 0 and ns % tm_s == 0 and t % tq == 0 and n_pages % pages == 0
    lambda_init = 0.8 - 0.6 * math.exp(-0.3 * 1)

    row = lambda v: v.reshape(1, -1).astype(F32)
    bf = lambda w: w.astype(BF16)

    wd = w_mla_down[0]
    r0 = q_lora + kv_lora
    half = rope // 2
    wd_ext = bf(jnp.concatenate([wd, wd[:, r0 + half:r0 + rope], wd[:, r0:r0 + half]], axis=1))
    wuq = w_mla_uq[0].reshape(q_lora, heads, nope + rope)
    wuq_ext = bf(jnp.concatenate([
        wuq[:, :, :nope].reshape(q_lora, heads * nope),
        wuq[:, :, nope:].reshape(q_lora, heads * rope),
        jnp.concatenate([wuq[:, :, nope + half:], wuq[:, :, nope:nope + half]], axis=-1).reshape(q_lora, heads * rope),
    ], axis=1))
    wuk_t = bf(jnp.transpose(w_mla_uk[0], (1, 2, 0)))
    wuv = bf(jnp.transpose(w_mla_uv[0], (1, 0, 2)))
    wo_mla = bf(w_mla_o[0])
    wqkv = bf(w_diff_qkv[0])
    wo_diff = bf(w_diff_o[0])
    wup = bf(w_ff_up)
    wdn = bf(w_ff_down)
    lamp = jnp.stack([lam_q1[0], lam_k1[0], lam_q2[0], lam_k2[0]]).astype(F32)
    gsub = row(g_diff_sub[0])
    mla_dims = (heads, q_lora, kv_lora, nope, rope)

    cos_p, sin_p = _rope_tables(jnp.arange(t, dtype=jnp.int32), half, heads)
    pos_s = past + (jnp.arange(ns, dtype=jnp.int32) % n_new)
    cos_s, sin_s = _rope_tables(pos_s, half, heads)
    bias_p = _bias_by_distance(rel_bias, 2 * tq, tq + 1, t + tq)
    d0 = _toeplitz(bias_p, tq, 0)
    d1 = _toeplitz(bias_p, tq, tq)

    hp = x_prompt.reshape(nb * t, dm)
    hs = x_sample.reshape(ns, dm)

    c_p, krt_p, kcat_p, q_p = _mla_pre(hp, row(g_mix[0]), wd_ext, row(g_mla_q[0]), row(g_mla_kv[0]), wuq_ext,
                                       wuk_t, cos_p, sin_p, nb=nb, tm=tm_p, dims=mla_dims)
    hp = _mla_attn_prompt(q_p, kcat_p, hp, wuv, wo_mla, nb=nb, t=t, tq=tq, kv_lora=kv_lora)
    hp = _ffn(hp, row(g_ffn[0]), wup[0], wdn[0], row(g_final), tm=tm_p, final_norm=False)

    c_s, krt_s, _, q_s = _mla_pre(hs, row(g_mix[0]), wd_ext, row(g_mla_q[0]), row(g_mla_kv[0]), wuq_ext,
                                  wuk_t, cos_s, sin_s, nb=1, tm=tm_s, dims=mla_dims)
    kw = kv_lora + LANES
    q_dec = q_s.reshape(heads, db, n_new, kw).transpose(1, 0, 2, 3).reshape(db, heads * n_new, kw)
    kr_s = krt_s[0].T
    knew = jnp.concatenate([c_s, kr_s, jnp.zeros((ns, LANES - rope), F32)], axis=-1).reshape(db, n_new, kw)
    knew = jnp.pad(knew, ((0, 0), (0, 8 - n_new), (0, 0)))
    o_lat = _mla_decode(page_table, q_dec, knew, cache_mla_ckv[0], jnp.swapaxes(cache_mla_krope[0], 1, 2),
                        pages=pages, n_new=n_new)
    o_lat = o_lat.reshape(db, heads, n_new, kv_lora).transpose(0, 2, 1, 3).reshape(ns, heads * kv_lora)
    hs = _mla_post(o_lat, hs, wuv, wo_mla, tm=tm_s)
    hs = _ffn(hs, row(g_ffn[0]), wup[0], wdn[0], row(g_final), tm=tm_s, final_norm=False)

    qd_p, k_p, v_p, kb_p, vb_p = _diff_pre(hp, row(g_mix[1]), wqkv, tm=tm_p, heads=dheads, hd2=hd2, nk=nk)
    hp = _diff_attn_prompt(qd_p, kb_p, vb_p, d0, d1, hp, lamp, gsub, wo_diff, nb=nb, t=t, tq=tq,
                           kv_heads=kv_heads, lambda_init=lambda_init)
    y_p = _ffn(hp, row(g_ffn[1]), wup[1], wdn[1], row(g_final), tm=tm_p, final_norm=True)

    qd_s, k_s, v_s, _, _ = _diff_pre(hs, row(g_mix[1]), wqkv, tm=tm_s, heads=dheads, hd2=hd2, nk=nk)
    q6 = qd_s.reshape(kv_heads, group, db, n_new, 2, hd2 // 2).transpose(2, 0, 4, 1, 3, 5)
    place = jnp.eye(2 * kv_heads, dtype=BF16).reshape(kv_heads, 2, 2 * kv_heads)
    q_dec = (q6[..., None, :] * place[None, :, :, None, None, :, None]).reshape(db, 2 * dheads * n_new, nk)
    pad_new = lambda a: jnp.pad(a.reshape(db, n_new, nk), ((0, 0), (0, 8 - n_new), (0, 0)))
    bias_s = _bias_by_distance(rel_bias, page + n_new, page + 1, past + n_new)
    b_last = jnp.stack([bias_s[:, tk + 1:tk + 1 + page][:, ::-1] for tk in range(n_new)], axis=1)
    b_new = jnp.stack([jnp.pad(bias_s[:, :tk + 1][:, ::-1], ((0, 0), (0, LANES - tk - 1))) for tk in range(n_new)],
                      axis=1)

    def per_row(b):
        b = jnp.broadcast_to(b.reshape(kv_heads, 1, group, n_new, -1), (kv_heads, 2, group, n_new, b.shape[-1]))
        return b.reshape(2 * dheads * n_new, -1)

    b_last, b_new = per_row(b_last), per_row(b_new)
    o_d = _diff_decode(page_table, q_dec, pad_new(k_s), pad_new(v_s), b_last, b_new, lamp, gsub,
                       cache_diff_k[0].reshape(-1, page * kv_heads, hd2),
                       cache_diff_v[0].reshape(-1, page * kv_heads, hd2),
                       pages=pages, kv_heads=kv_heads, group=group, n_new=n_new, lambda_init=lambda_init)
    o_d = o_d.reshape(db, kv_heads, group, n_new, hd2).transpose(0, 3, 1, 2, 4).reshape(ns, dheads * hd2)
    hs = _proj_resid(o_d, hs, wo_diff, tm=tm_s)
    y_s = _ffn(hs, row(g_ffn[1]), wup[1], wdn[1], row(g_final), tm=tm_s, final_norm=True)

    return (y_p.reshape(nb, t, dm), y_s.reshape(db, n_new, dm),
            c_p.reshape(1, nb, t, kv_lora), jnp.swapaxes(krt_p, 1, 2)[None],
            k_p.reshape(1, nb, t, kv_heads, hd2), v_p.reshape(1, nb, t, kv_heads, hd2),
            c_s.reshape(1, db, n_new, kv_lora), kr_s.reshape(1, db, n_new, rope),
            k_s.reshape(1, db, n_new, kv_heads, hd2), v_s.reshape(1, db, n_new, kv_heads, hd2))
```

```python
import functools
import math

import numpy as np
import jax
import jax.numpy as jnp
from jax import lax
from jax.experimental import pallas as pl
from jax.experimental.pallas import tpu as pltpu

F32 = jnp.float32
BF16 = jnp.bfloat16

ROPE_THETA = 10000.0
REL_BUCKETS = 32
REL_MAX_DIST = 128
NORM_EPS = 1e-6
SUBLN_EPS = 1e-5
NEG = -1e30
LOG2E = math.log2(math.e)

LANES = 128
ROW_TILE = 512
ATTN_TILE = 256
PAGES_PER_STEP = 32
RING_SLOTS = 3
VMEM_LIMIT = 56 * 1024 * 1024


def _cparams(*sem):
    return pltpu.CompilerParams(dimension_semantics=sem, vmem_limit_bytes=VMEM_LIMIT)


def _rms(x, g, eps):
    return x * lax.rsqrt(jnp.mean(x * x, axis=-1, keepdims=True) + eps) * g


def _dot(a, b):
    return jnp.dot(a, b, preferred_element_type=F32)


def _dot_nt(a, b):
    return lax.dot_general(a, b, (((1,), (1,)), ((), ())), preferred_element_type=F32)


def _lane_block_sum(p):
    out = p[:, :LANES]
    for j in range(1, p.shape[1] // LANES):
        out = out + p[:, j * LANES:(j + 1) * LANES]
    return out


def _softmax_step(s, v, m_sc, l_sc, acc_sc):
    m_prev = m_sc[...]
    m_new = jnp.maximum(m_prev, jnp.max(s, axis=-1, keepdims=True))
    alpha = jnp.exp2(m_prev - m_new)
    p = jnp.exp2(s - jnp.tile(m_new, (1, s.shape[1] // LANES)))
    l_sc[...] = alpha * l_sc[...] + _lane_block_sum(p)
    acc_sc[...] = jnp.tile(alpha, (1, acc_sc.shape[1] // LANES)) * acc_sc[...] + _dot(p.astype(BF16), v)
    m_sc[...] = m_new


def _softmax_init(m_sc, l_sc, acc_sc):
    m_sc[...] = jnp.full(m_sc.shape, NEG, F32)
    l_sc[...] = jnp.zeros(l_sc.shape, F32)
    acc_sc[...] = jnp.zeros(acc_sc.shape, F32)


def _mla_pre_body(h_ref, g_ref, wd_ref, gq_ref, gkv_ref, wuq_ref, wuk_ref, cos_ref, sin_ref,
                  c_ref, kr_ref, kcat_ref, q_ref, *, scale, heads, q_lora, kv_lora, nope, rope):
    tm = h_ref.shape[0]
    a = _rms(h_ref[...], g_ref[...], NORM_EPS).astype(BF16)
    d = _dot(a, wd_ref[...])
    cq = _rms(d[:, :q_lora], gq_ref[...], NORM_EPS)
    c = _rms(d[:, q_lora:q_lora + kv_lora], gkv_ref[...], NORM_EPS)
    cosq = cos_ref[...]
    sinq = sin_ref[...]
    r0 = q_lora + kv_lora
    kr = d[:, r0:r0 + rope] * cosq[:, :rope] + d[:, r0 + rope:r0 + 2 * rope] * sinq[:, :rope]
    c_ref[...] = c
    pad = jnp.zeros((tm, LANES - rope), F32)
    kr_ref[...] = jnp.concatenate([kr, pad], axis=-1).T[:rope]
    kcat_ref[...] = jnp.concatenate([c, kr, pad], axis=-1).astype(BF16)
    q = _dot(cq.astype(BF16), wuq_ref[...])
    hn = heads * nope
    hr = heads * rope
    qr = (q[:, hn:hn + hr] * cosq + q[:, hn + hr:hn + 2 * hr] * sinq) * scale
    for h in range(heads):
        ql = _dot(q[:, h * nope:(h + 1) * nope].astype(BF16), wuk_ref[h]) * scale
        q_ref[h] = jnp.concatenate([ql, qr[:, h * rope:(h + 1) * rope], pad], axis=-1).astype(BF16)


def _mla_pre(h, g, wd, gq, gkv, wuq, wuk, cosq, sinq, *, nb, tm, dims):
    n, dm = h.shape
    t = cosq.shape[0]
    nt = t // tm
    heads, q_lora, kv_lora, nope, rope = dims
    kw = kv_lora + LANES
    row = lambda ti, b: (b * nt + ti, 0)
    const2 = lambda ti, b: (0, 0)
    body = functools.partial(_mla_pre_body, scale=LOG2E * (nope + rope) ** -0.5, heads=heads, q_lora=q_lora,
                             kv_lora=kv_lora, nope=nope, rope=rope)
    return pl.pallas_call(
        body,
        grid=(nt, nb),
        in_specs=[
            pl.BlockSpec((tm, dm), row),
            pl.BlockSpec(g.shape, const2),
            pl.BlockSpec(wd.shape, const2),
            pl.BlockSpec(gq.shape, const2),
            pl.BlockSpec(gkv.shape, const2),
            pl.BlockSpec(wuq.shape, const2),
            pl.BlockSpec(wuk.shape, lambda ti, b: (0, 0, 0)),
            pl.BlockSpec((tm, cosq.shape[1]), lambda ti, b: (ti, 0)),
            pl.BlockSpec((tm, sinq.shape[1]), lambda ti, b: (ti, 0)),
        ],
        out_specs=[
            pl.BlockSpec((tm, kv_lora), row),
            pl.BlockSpec((None, rope, tm), lambda ti, b: (b, 0, ti)),
            pl.BlockSpec((tm, kw), row),
            pl.BlockSpec((heads, tm, kw), lambda ti, b: (0, b * nt + ti, 0)),
        ],
        out_shape=[
            jax.ShapeDtypeStruct((n, kv_lora), F32),
            jax.ShapeDtypeStruct((nb, rope, t), F32),
            jax.ShapeDtypeStruct((n, kw), BF16),
            jax.ShapeDtypeStruct((heads, n, kw), BF16),
        ],
        compiler_params=_cparams("arbitrary", "arbitrary"),
        name="mla_pre",
    )(h, g, wd, gq, gkv, wuq, wuk, cosq, sinq)


def _mla_out(o_heads, wuv_ref, wo_ref, resid):
    ov = [_dot(o.astype(BF16), wuv_ref[h]) for h, o in enumerate(o_heads)]
    o = jnp.concatenate(ov, axis=-1).astype(BF16)
    return resid + _dot(o, wo_ref[...])


def _mla_attn_body(q_ref, k_ref, h_ref, wuv_ref, wo_ref, o_ref, m_sc, l_sc, acc_sc, *, heads, kv_lora, tq):
    i = pl.program_id(1)
    q = q_ref[...].reshape(heads * tq, q_ref.shape[2])
    _softmax_init(m_sc, l_sc, acc_sc)

    def keys(j):
        return k_ref[pl.ds(pl.multiple_of(j * tq, tq), tq), :]

    def full_step(j, s):
        s_next = _dot_nt(q, keys(j + 1))
        _softmax_step(s, keys(j)[:, :kv_lora], m_sc, l_sc, acc_sc)
        return s_next

    s = lax.fori_loop(0, i, full_step, _dot_nt(q, keys(0)))
    row = lax.broadcasted_iota(jnp.int32, (tq, tq), 0)
    col = lax.broadcasted_iota(jnp.int32, (tq, tq), 1)
    s = jnp.where((col <= row)[None], s.reshape(heads, tq, tq), NEG).reshape(heads * tq, tq)
    _softmax_step(s, keys(i)[:, :kv_lora], m_sc, l_sc, acc_sc)

    o = acc_sc[...] / jnp.sum(l_sc[...], axis=-1, keepdims=True)
    o_heads = [o[h * tq:(h + 1) * tq] for h in range(heads)]
    o_ref[...] = _mla_out(o_heads, wuv_ref, wo_ref, h_ref[...])


def _mla_attn_prompt(q, kcat, h, wuv, wo, *, nb, t, tq, kv_lora):
    heads, n, kw = q.shape
    dm = h.shape[1]
    nq = t // tq
    body = functools.partial(_mla_attn_body, heads=heads, kv_lora=kv_lora, tq=tq)
    return pl.pallas_call(
        body,
        grid=(nb, nq),
        in_specs=[
            pl.BlockSpec((heads, tq, kw), lambda b, i: (0, b * nq + i, 0)),
            pl.BlockSpec((t, kw), lambda b, i: (b, 0)),
            pl.BlockSpec((tq, dm), lambda b, i: (b * nq + i, 0)),
            pl.BlockSpec(wuv.shape, lambda b, i: (0, 0, 0)),
            pl.BlockSpec(wo.shape, lambda b, i: (0, 0)),
        ],
        out_specs=pl.BlockSpec((tq, dm), lambda b, i: (b * nq + i, 0)),
        out_shape=jax.ShapeDtypeStruct((n, dm), F32),
        scratch_shapes=[
            pltpu.VMEM((heads * tq, LANES), F32),
            pltpu.VMEM((heads * tq, LANES), F32),
            pltpu.VMEM((heads * tq, kv_lora), F32),
        ],
        compiler_params=_cparams("arbitrary", "arbitrary"),
        name="mla_attn_prompt",
    )(q, kcat, h, wuv, wo)


def _mla_post_body(o_ref, h_ref, wuv_ref, wo_ref, out_ref, *, heads, kv_lora):
    o = o_ref[...]
    o_heads = [o[:, h * kv_lora:(h + 1) * kv_lora] for h in range(heads)]
    out_ref[...] = _mla_out(o_heads, wuv_ref, wo_ref, h_ref[...])


def _mla_post(o_lat, h, wuv, wo, *, tm):
    n, dm = h.shape
    heads, kv_lora, _ = wuv.shape
    body = functools.partial(_mla_post_body, heads=heads, kv_lora=kv_lora)
    return pl.pallas_call(
        body,
        grid=(n // tm,),
        in_specs=[
            pl.BlockSpec((tm, o_lat.shape[1]), lambda i: (i, 0)),
            pl.BlockSpec((tm, dm), lambda i: (i, 0)),
            pl.BlockSpec(wuv.shape, lambda i: (0, 0, 0)),
            pl.BlockSpec(wo.shape, lambda i: (0, 0)),
        ],
        out_specs=pl.BlockSpec((tm, dm), lambda i: (i, 0)),
        out_shape=jax.ShapeDtypeStruct((n, dm), F32),
        compiler_params=_cparams("arbitrary"),
        name="mla_post",
    )(o_lat, h, wuv, wo)


def _ffn_body(h_ref, g_ref, wup_ref, wdn_ref, gf_ref, o_ref, *, chunk, final_norm):
    x = h_ref[...]
    a = _rms(x, g_ref[...], NORM_EPS).astype(BF16)
    acc = x
    for j in range(wup_ref.shape[1] // chunk):
        u = jnp.maximum(_dot(a, wup_ref[:, j * chunk:(j + 1) * chunk]), 0.0)
        acc = acc + _dot((u * u).astype(BF16), wdn_ref[j * chunk:(j + 1) * chunk, :])
    if final_norm:
        acc = _rms(acc, gf_ref[...], NORM_EPS)
    o_ref[...] = acc


def _ffn(h, g, wup, wdn, gf, *, tm, final_norm):
    n, dm = h.shape
    body = functools.partial(_ffn_body, chunk=min(1024, wup.shape[1]), final_norm=final_norm)
    const = lambda i: (0, 0)
    return pl.pallas_call(
        body,
        grid=(n // tm,),
        in_specs=[
            pl.BlockSpec((tm, dm), lambda i: (i, 0)),
            pl.BlockSpec(g.shape, const),
            pl.BlockSpec(wup.shape, const, pipeline_mode=pl.Buffered(1)),
            pl.BlockSpec(wdn.shape, const, pipeline_mode=pl.Buffered(1)),
            pl.BlockSpec(gf.shape, const),
        ],
        out_specs=pl.BlockSpec((tm, dm), lambda i: (i, 0)),
        out_shape=jax.ShapeDtypeStruct((n, dm), F32),
        compiler_params=_cparams("arbitrary"),
        name="ffn_final" if final_norm else "ffn",
    )(h, g, wup, wdn, gf)


def _diff_pre_body(h_ref, g_ref, w_ref, q_ref, k_ref, v_ref, kb_ref, vb_ref, *, scale, heads, hd2, nk):
    a = _rms(h_ref[...], g_ref[...], NORM_EPS).astype(BF16)
    proj = _dot(a, w_ref[...])
    nq = heads * hd2
    for h in range(heads):
        q_ref[h] = (proj[:, h * hd2:(h + 1) * hd2] * scale).astype(BF16)
    k = proj[:, nq:nq + nk]
    v = proj[:, nq + nk:nq + 2 * nk]
    kv_heads = nk // hd2
    tm = h_ref.shape[0]
    for n in range(kv_heads):
        k_ref[pl.ds(n, tm, stride=kv_heads), :] = k[:, n * hd2:(n + 1) * hd2]
        v_ref[pl.ds(n, tm, stride=kv_heads), :] = v[:, n * hd2:(n + 1) * hd2]
    kb_ref[...] = k.astype(BF16)
    vb_ref[...] = v.astype(BF16)


def _diff_pre(h, g, w, *, tm, heads, hd2, nk):
    n, dm = h.shape
    body = functools.partial(_diff_pre_body, scale=LOG2E * (hd2 // 2) ** -0.5, heads=heads, hd2=hd2, nk=nk)
    row = lambda i: (i, 0)
    return pl.pallas_call(
        body,
        grid=(n // tm,),
        in_specs=[
            pl.BlockSpec((tm, dm), row),
            pl.BlockSpec(g.shape, lambda i: (0, 0)),
            pl.BlockSpec(w.shape, lambda i: (0, 0)),
        ],
        out_specs=[
            pl.BlockSpec((heads, tm, hd2), lambda i: (0, i, 0)),
            pl.BlockSpec((tm * nk // hd2, hd2), row),
            pl.BlockSpec((tm * nk // hd2, hd2), row),
            pl.BlockSpec((tm, nk), row),
            pl.BlockSpec((tm, nk), row),
        ],
        out_shape=[
            jax.ShapeDtypeStruct((heads, n, hd2), BF16),
            jax.ShapeDtypeStruct((n * nk // hd2, hd2), F32),
            jax.ShapeDtypeStruct((n * nk // hd2, hd2), F32),
            jax.ShapeDtypeStruct((n, nk), BF16),
            jax.ShapeDtypeStruct((n, nk), BF16),
        ],
        compiler_params=_cparams("arbitrary"),
        name="diff_pre",
    )(h, g, w)


def _lambda_value(lam_ref, lambda_init):
    lp = lam_ref[...]
    e1 = jnp.exp(jnp.sum(lp[0:1] * lp[1:2], axis=-1, keepdims=True))
    e2 = jnp.exp(jnp.sum(lp[2:3] * lp[3:4], axis=-1, keepdims=True))
    return e1 - e2 + lambda_init


def _diff_attn_body(q_ref, k_ref, v_ref, d0_ref, d1_ref, h_ref, lam_ref, gsub_ref, wo_ref, o_ref,
                    m_sc, l_sc, acc_sc, *, kv_heads, group, hd2, tq, lambda_init):
    i = pl.program_id(1)
    lam = _lambda_value(lam_ref, lambda_init)
    rows = group * tq
    first_half = lax.broadcasted_iota(jnp.int32, (rows, hd2), 1) < hd2 // 2
    row = lax.broadcasted_iota(jnp.int32, (tq, tq), 0)
    col = lax.broadcasted_iota(jnp.int32, (tq, tq), 1)
    causal = (col <= row)[None, None]
    qzs = []
    for n in range(kv_heads):
        qn = q_ref[n * group:(n + 1) * group].reshape(rows, hd2)
        zero = jnp.zeros_like(qn)
        qzs.append(jnp.concatenate([jnp.where(first_half, qn, zero), jnp.where(first_half, zero, qn)], axis=0))
    _softmax_init(m_sc, l_sc, acc_sc)

    def step(j, bias_ref, masked):
        start = pl.multiple_of(j * tq, tq)
        for n in range(kv_heads):
            lanes = slice(n * hd2, (n + 1) * hd2)
            s = _dot_nt(qzs[n], k_ref[pl.ds(start, tq), lanes])
            if bias_ref is not None:
                s = s.reshape(2, group, tq, tq) + bias_ref[n * group:(n + 1) * group][None]
                if masked:
                    s = jnp.where(causal, s, NEG)
                s = s.reshape(2 * rows, tq)
            _softmax_step(s, v_ref[pl.ds(start, tq), lanes], m_sc.at[n], l_sc.at[n], acc_sc.at[n])

    def far_step(j, carry):
        step(j, None, False)
        return carry

    lax.fori_loop(0, jnp.maximum(i - 1, 0), far_step, 0)

    @pl.when(i >= 1)
    def _():
        step(i - 1, d1_ref, False)

    step(i, d0_ref, True)

    outs = []
    for n in range(kv_heads):
        o = acc_sc[n] / jnp.sum(l_sc[n], axis=-1, keepdims=True)
        o = o[:rows] - lam * o[rows:]
        o = _rms(o, gsub_ref[...], SUBLN_EPS) * (1.0 - lambda_init)
        outs.extend(o[g * tq:(g + 1) * tq] for g in range(group))
    attn = jnp.concatenate(outs, axis=-1).astype(BF16)
    o_ref[...] = h_ref[...] + _dot(attn, wo_ref[...])


def _diff_attn_prompt(q, kb, vb, d0, d1, h, lamp, gsub, wo, *, nb, t, tq, kv_heads, lambda_init):
    heads, n, hd2 = q.shape
    dm = h.shape[1]
    nq = t // tq
    group = heads // kv_heads
    nk = kb.shape[1]
    body = functools.partial(_diff_attn_body, kv_heads=kv_heads, group=group, hd2=hd2, tq=tq,
                             lambda_init=lambda_init)
    c2 = lambda b, i: (0, 0)
    c3 = lambda b, i: (0, 0, 0)
    return pl.pallas_call(
        body,
        grid=(nb, nq),
        in_specs=[
            pl.BlockSpec((heads, tq, hd2), lambda b, i: (0, b * nq + i, 0)),
            pl.BlockSpec((t, nk), lambda b, i: (b, 0)),
            pl.BlockSpec((t, nk), lambda b, i: (b, 0)),
            pl.BlockSpec(d0.shape, c3),
            pl.BlockSpec(d1.shape, c3),
            pl.BlockSpec((tq, dm), lambda b, i: (b * nq + i, 0)),
            pl.BlockSpec(lamp.shape, c2),
            pl.BlockSpec(gsub.shape, c2),
            pl.BlockSpec(wo.shape, c2),
        ],
        out_specs=pl.BlockSpec((tq, dm), lambda b, i: (b * nq + i, 0)),
        out_shape=jax.ShapeDtypeStruct((n, dm), F32),
        scratch_shapes=[
            pltpu.VMEM((kv_heads, 2 * group * tq, LANES), F32),
            pltpu.VMEM((kv_heads, 2 * group * tq, LANES), F32),
            pltpu.VMEM((kv_heads, 2 * group * tq, hd2), F32),
        ],
        compiler_params=_cparams("arbitrary", "arbitrary"),
        name="diff_attn_prompt",
    )(q, kb, vb, d0, d1, h, lamp, gsub, wo)


def _proj_resid_body(x_ref, h_ref, w_ref, o_ref):
    o_ref[...] = h_ref[...] + _dot(x_ref[...].astype(BF16), w_ref[...])


def _proj_resid(x, h, w, *, tm):
    n, dm = h.shape
    return pl.pallas_call(
        _proj_resid_body,
        grid=(n // tm,),
        in_specs=[
            pl.BlockSpec((tm, x.shape[1]), lambda i: (i, 0)),
            pl.BlockSpec((tm, dm), lambda i: (i, 0)),
            pl.BlockSpec(w.shape, lambda i: (0, 0)),
        ],
        out_specs=pl.BlockSpec((tm, dm), lambda i: (i, 0)),
        out_shape=jax.ShapeDtypeStruct((n, dm), F32),
        compiler_params=_cparams("arbitrary"),
        name="proj_resid",
    )(x, h, w)


def _new_token_update(qf, k_new, v_new, bias_new, valid, m_sc, l_sc, acc_sc):
    n_new = valid.shape[1]
    lane0 = lax.broadcasted_iota(jnp.int32, (qf.shape[0], LANES), 1) == 0
    s_cols = []
    for t in range(n_new):
        s_t = jnp.sum(qf * k_new[t:t + 1, :], axis=-1, keepdims=True)
        if bias_new is not None:
            s_t = s_t + bias_new[:, t:t + 1]
        s_cols.append(jnp.where(valid[:, t:t + 1], s_t, NEG))
    m_prev = m_sc[...]
    m_new = m_prev
    for s_t in s_cols:
        m_new = jnp.maximum(m_new, s_t)
    alpha = jnp.exp2(m_prev - m_new)
    l = alpha * l_sc[...]
    acc = jnp.tile(alpha, (1, acc_sc.shape[1] // LANES)) * acc_sc[...]
    for t, s_t in enumerate(s_cols):
        p_t = jnp.exp2(s_t - m_new)
        l = l + jnp.where(lane0, p_t, 0.0)
        acc = acc + jnp.tile(p_t, (1, acc_sc.shape[1] // LANES)) * v_new[t:t + 1, :]
    m_sc[...] = m_new
    l_sc[...] = l
    acc_sc[...] = acc


def _page_ring(pt_ref, srcs, bufs, sems, *, pages):
    nchunks = pl.num_programs(1)
    step = pl.program_id(0) * nchunks + pl.program_id(1)
    total = pl.num_programs(0) * nchunks
    ahead = RING_SLOTS - 1
    slot = lax.rem(step, RING_SLOTS)

    def copies(step_idx, slot_idx):
        out = []
        for p in range(pages):
            pg = pt_ref[step_idx * pages + p]
            for src, buf, sem in zip(srcs, bufs, sems):
                out.append(pltpu.make_async_copy(src.at[pg], buf.at[slot_idx, p], sem.at[slot_idx]))
        return out

    def start_all(step_idx, slot_idx):
        for n, cp in enumerate(copies(step_idx, slot_idx)):
            cp.start(priority=(n // len(srcs)) % 2)

    for first in range(ahead):
        @pl.when((step == 0) & (first < total))
        def _():
            start_all(first, first)

    @pl.when(step + ahead < total)
    def _():
        start_all(step + ahead, lax.rem(step + ahead, RING_SLOTS))

    for cp in copies(step, slot):
        cp.wait()
    return slot


def _mla_decode_body(pt_ref, q_ref, knew_ref, ckv_hbm, kr_hbm, o_ref, ckv_pg, kr_pg, cbuf, krbuf, m_sc, l_sc, acc_sc,
                     sem_c, sem_r, *, pages, page, kv_lora, rope, n_new):
    c = pl.program_id(1)
    slot = _page_ring(pt_ref, (ckv_hbm, kr_hbm), (ckv_pg, kr_pg), (sem_c, sem_r), pages=pages)

    @pl.when(c == 0)
    def _():
        _softmax_init(m_sc, l_sc, acc_sc)

    krbuf[rope:, :] = jnp.zeros((krbuf.shape[0] - rope, krbuf.shape[1]), BF16)
    for p in range(pages):
        cbuf[p * page:(p + 1) * page, :] = ckv_pg[slot, p].astype(BF16)
        krbuf[:rope, p * page:(p + 1) * page] = kr_pg[slot, p].astype(BF16)
    q = q_ref[...]
    ck = cbuf[...]
    s = _dot_nt(q[:, :kv_lora], ck) + _dot(q[:, kv_lora:], krbuf[...])
    _softmax_step(s, ck, m_sc, l_sc, acc_sc)

    @pl.when(c == pl.num_programs(1) - 1)
    def _():
        r = q.shape[0]
        k_new = knew_ref[...]
        t_row = lax.broadcasted_iota(jnp.int32, (r, n_new), 0) % n_new
        t_col = lax.broadcasted_iota(jnp.int32, (r, n_new), 1)
        _new_token_update(q.astype(F32), k_new, k_new[:, :kv_lora], None, t_col <= t_row, m_sc, l_sc, acc_sc)
        o_ref[...] = acc_sc[...] / jnp.sum(l_sc[...], axis=-1, keepdims=True)


def _mla_decode(page_table, q, knew, cache_ckv, cache_kr, *, pages, n_new):
    db, r, kw = q.shape
    n_pages = page_table.shape[1]
    page, kv_lora = cache_ckv.shape[1:]
    rope = cache_kr.shape[1]
    nchunks = n_pages // pages
    pt = page_table.reshape(-1)

    body = functools.partial(_mla_decode_body, pages=pages, page=page, kv_lora=kv_lora, rope=rope, n_new=n_new)
    grid_spec = pltpu.PrefetchScalarGridSpec(
        num_scalar_prefetch=1,
        grid=(db, nchunks),
        in_specs=[
            pl.BlockSpec((None, r, kw), lambda b, c, pt_ref: (b, 0, 0)),
            pl.BlockSpec((None,) + knew.shape[1:], lambda b, c, pt_ref: (b, 0, 0)),
            pl.BlockSpec(memory_space=pl.ANY),
            pl.BlockSpec(memory_space=pl.ANY),
        ],
        out_specs=pl.BlockSpec((None, r, kv_lora), lambda b, c, pt_ref: (b, 0, 0)),
        scratch_shapes=[
            pltpu.VMEM((RING_SLOTS, pages, page, kv_lora), F32),
            pltpu.VMEM((RING_SLOTS, pages, rope, page), F32),
            pltpu.VMEM((pages * page, kv_lora), BF16),
            pltpu.VMEM((kw - kv_lora, pages * page), BF16),
            pltpu.VMEM((r, LANES), F32),
            pltpu.VMEM((r, LANES), F32),
            pltpu.VMEM((r, kv_lora), F32),
            pltpu.SemaphoreType.DMA((RING_SLOTS,)),
            pltpu.SemaphoreType.DMA((RING_SLOTS,)),
        ],
    )
    return pl.pallas_call(
        body,
        grid_spec=grid_spec,
        out_shape=jax.ShapeDtypeStruct((db, r, kv_lora), F32),
        compiler_params=_cparams("arbitrary", "arbitrary"),
        name="mla_decode",
    )(pt, q, knew, cache_ckv, cache_kr)


def _diff_decode_body(pt_ref, q_ref, knew_ref, vnew_ref, blast_ref, bnew_ref, lam_ref, gsub_ref, k_hbm, v_hbm,
                      o_ref, k_pg, v_pg, kbuf, vbuf, s_sc, m_sc, l_sc, acc_sc, sem_k, sem_v, *,
                      pages, page, kv_heads, group, hd2, n_new, lambda_init):
    c = pl.program_id(1)
    last = pl.num_programs(1) - 1
    slot = _page_ring(pt_ref, (k_hbm, v_hbm), (k_pg, v_pg), (sem_k, sem_v), pages=pages)

    @pl.when(c == 0)
    def _():
        _softmax_init(m_sc, l_sc, acc_sc)

    for p in range(pages):
        rows = slice(p * page, (p + 1) * page)
        for n in range(kv_heads):
            lanes = slice(n * hd2, (n + 1) * hd2)
            kbuf[rows, lanes] = k_pg[slot, p, pl.ds(n, page, stride=kv_heads), :].astype(BF16)
            vbuf[rows, lanes] = v_pg[slot, p, pl.ds(n, page, stride=kv_heads), :].astype(BF16)
    q = q_ref[...]
    s_sc[...] = _dot_nt(q, kbuf[...])

    @pl.when(c == last)
    def _():
        cols = slice((pages - 1) * page, pages * page)
        s_sc[:, cols] = s_sc[:, cols] + blast_ref[...]

    _softmax_step(s_sc[...], vbuf[...], m_sc, l_sc, acc_sc)

    @pl.when(c == last)
    def _():
        r = q.shape[0]
        t_row = lax.broadcasted_iota(jnp.int32, (r, n_new), 0) % n_new
        t_col = lax.broadcasted_iota(jnp.int32, (r, n_new), 1)
        _new_token_update(q.astype(F32), knew_ref[...], vnew_ref[...], bnew_ref[...], t_col <= t_row,
                          m_sc, l_sc, acc_sc)
        o = acc_sc[...] / jnp.sum(l_sc[...], axis=-1, keepdims=True)
        lam = _lambda_value(lam_ref, lambda_init)
        gr = group * n_new
        outs = []
        for n in range(kv_heads):
            o1 = o[(2 * n) * gr:(2 * n + 1) * gr, n * hd2:(n + 1) * hd2]
            o2 = o[(2 * n + 1) * gr:(2 * n + 2) * gr, n * hd2:(n + 1) * hd2]
            outs.append(o1 - lam * o2)
        on = jnp.concatenate(outs, axis=0)
        o_ref[...] = _rms(on, gsub_ref[...], SUBLN_EPS) * (1.0 - lambda_init)


def _diff_decode(page_table, q, knew, vnew, blast, bnew, lamp, gsub, cache_k, cache_v, *, pages, kv_heads,
                 group, n_new, lambda_init):
    db, r, nk = q.shape
    n_pages = page_table.shape[1]
    hd2 = nk // kv_heads
    page = cache_k.shape[1] // kv_heads
    nchunks = n_pages // pages
    pt = page_table.reshape(-1)

    per_b = lambda b, c, pt_ref: (b, 0, 0)
    const = lambda b, c, pt_ref: (0, 0)
    body = functools.partial(_diff_decode_body, pages=pages, page=page, kv_heads=kv_heads, group=group,
                             hd2=hd2, n_new=n_new, lambda_init=lambda_init)
    r_out = kv_heads * group * n_new
    grid_spec = pltpu.PrefetchScalarGridSpec(
        num_scalar_prefetch=1,
        grid=(db, nchunks),
        in_specs=[
            pl.BlockSpec((None, r, nk), per_b),
            pl.BlockSpec((None,) + knew.shape[1:], per_b),
            pl.BlockSpec((None,) + vnew.shape[1:], per_b),
            pl.BlockSpec(blast.shape, const),
            pl.BlockSpec(bnew.shape, const),
            pl.BlockSpec(lamp.shape, const),
            pl.BlockSpec(gsub.shape, const),
            pl.BlockSpec(memory_space=pl.ANY),
            pl.BlockSpec(memory_space=pl.ANY),
        ],
        out_specs=pl.BlockSpec((None, r_out, hd2), per_b),
        scratch_shapes=[
            pltpu.VMEM((RING_SLOTS, pages, page * kv_heads, hd2), F32),
            pltpu.VMEM((RING_SLOTS, pages, page * kv_heads, hd2), F32),
            pltpu.VMEM((pages * page, nk), BF16),
            pltpu.VMEM((pages * page, nk), BF16),
            pltpu.VMEM((r, pages * page), F32),
            pltpu.VMEM((r, LANES), F32),
            pltpu.VMEM((r, LANES), F32),
            pltpu.VMEM((r, nk), F32),
            pltpu.SemaphoreType.DMA((RING_SLOTS,)),
            pltpu.SemaphoreType.DMA((RING_SLOTS,)),
        ],
    )
    return pl.pallas_call(
        body,
        grid_spec=grid_spec,
        out_shape=jax.ShapeDtypeStruct((db, r_out, hd2), F32),
        compiler_params=_cparams("arbitrary", "arbitrary"),
        name="diff_decode",
    )(pt, q, knew, vnew, blast, bnew, lamp, gsub, cache_k, cache_v)


def _rope_tables(pos, half, heads):
    inv = ROPE_THETA ** (-jnp.arange(half, dtype=F32) / half)
    ang = pos.astype(F32)[:, None] * inv[None, :]
    cos, sin = jnp.cos(ang), jnp.sin(ang)
    cos2 = jnp.concatenate([cos, cos], axis=-1)
    sin2 = jnp.concatenate([-sin, sin], axis=-1)
    return jnp.tile(cos2, (1, heads)), jnp.tile(sin2, (1, heads))


def _bucket_of_distance(n):
    max_exact = REL_BUCKETS // 2
    nf = np.maximum(n, 1).astype(np.float32)
    ratio = np.log(nf / np.float32(max_exact)) / np.float32(math.log(REL_MAX_DIST / max_exact))
    large = max_exact + (ratio * np.float32(REL_BUCKETS - max_exact)).astype(np.int32)
    large = np.minimum(large, REL_BUCKETS - 1)
    return np.where(n < max_exact, n, large)


def _bias_by_distance(rel_bias, n, far_from, max_dist):
    far = _bucket_of_distance(np.arange(far_from, max_dist + 1))
    assert (far == far[0]).all(), "relative-position bias must be constant beyond the near tiles"
    select = np.zeros((n, REL_BUCKETS), np.float32)
    select[np.arange(n), _bucket_of_distance(np.arange(n))] = 1.0
    select[:, int(far[0])] -= 1.0
    picked = jnp.dot(jnp.asarray(select), rel_bias.astype(F32), precision=lax.Precision.HIGHEST)
    return LOG2E * picked.T


def _toeplitz(v, n, offset):
    heads = v.shape[0]
    length = 2 * n
    lo = n - 1 - offset
    u = jnp.pad(v, ((0, 0), (lo, 0)))[:, :length] if lo >= 0 else jnp.pad(v[:, -lo:], ((0, 0), (0, -lo)))
    x = jnp.tile(u, (1, n + 1))[:, :n * (length + 1)].reshape(heads, n, length + 1)[:, :, :n]
    return x[:, :, ::-1]


def kernel(x_prompt, x_sample, cache_mla_ckv, cache_mla_krope, cache_diff_k, cache_diff_v, page_table, g_mix, g_ffn, g_final, w_mla_down, g_mla_q, g_mla_kv, w_mla_uq, w_mla_uk, w_mla_uv, w_mla_o, w_diff_qkv, lam_q1, lam_k1, lam_q2, lam_k2, g_diff_sub, w_diff_o, rel_bias, w_ff_up, w_ff_down):
    nb, t, dm = x_prompt.shape
    db, n_new, _ = x_sample.shape
    n_pages = page_table.shape[1]
    page = cache_mla_ckv.shape[2]
    past = n_pages * page
    kv_lora, heads, nope = w_mla_uk.shape[1:]
    q_lora = g_mla_q.shape[1]
    rope = cache_mla_krope.shape[3]
    kv_heads, hd2 = cache_diff_k.shape[3:]
    nk = kv_heads * hd2
    dheads = rel_bias.shape[1]
    group = dheads // kv_heads
    assert g_mix.shape[0] == 2 and w_mla_down.shape[0] == 1 and w_diff_qkv.shape[0] == 1
    assert heads * rope == 4 * LANES and kv_lora % LANES == 0 and hd2 == LANES and rope * 2 == LANES

    tm_p = min(ROW_TILE, t)
    ns = db * n_new
    tm_s = min(ROW_TILE, ns)
    tq = min(ATTN_TILE, t)
    pages = min(PAGES_PER_STEP, n_pages)
    assert t % tm_p == 0 and ns % tm_s == 0 and t % tq == 0 and n_pages % pages == 0
    lambda_init = 0.8 - 0.6 * math.exp(-0.3 * 1)

    row = lambda v: v.reshape(1, -1).astype(F32)
    bf = lambda w: w.astype(BF16)

    wd = w_mla_down[0]
    r0 = q_lora + kv_lora
    half = rope // 2
    wd_ext = bf(jnp.concatenate([wd, wd[:, r0 + half:r0 + rope], wd[:, r0:r0 + half]], axis=1))
    wuq = w_mla_uq[0].reshape(q_lora, heads, nope + rope)
    wuq_ext = bf(jnp.concatenate([
        wuq[:, :, :nope].reshape(q_lora, heads * nope),
        wuq[:, :, nope:].reshape(q_lora, heads * rope),
        jnp.concatenate([wuq[:, :, nope + half:], wuq[:, :, nope:nope + half]], axis=-1).reshape(q_lora, heads * rope),
    ], axis=1))
    wuk_t = bf(jnp.transpose(w_mla_uk[0], (1, 2, 0)))
    wuv = bf(jnp.transpose(w_mla_uv[0], (1, 0, 2)))
    wo_mla = bf(w_mla_o[0])
    wqkv = bf(w_diff_qkv[0])
    wo_diff = bf(w_diff_o[0])
    wup = bf(w_ff_up)
    wdn = bf(w_ff_down)
    lamp = jnp.stack([lam_q1[0], lam_k1[0], lam_q2[0], lam_k2[0]]).astype(F32)
    gsub = row(g_diff_sub[0])
    mla_dims = (heads, q_lora, kv_lora, nope, rope)

    cos_p, sin_p = _rope_tables(jnp.arange(t, dtype=jnp.int32), half, heads)
    pos_s = past + (jnp.arange(ns, dtype=jnp.int32) % n_new)
    cos_s, sin_s = _rope_tables(pos_s, half, heads)
    bias_p = _bias_by_distance(rel_bias, 2 * tq, tq + 1, t + tq)
    d0 = _toeplitz(bias_p, tq, 0)
    d1 = _toeplitz(bias_p, tq, tq)

    hp = x_prompt.reshape(nb * t, dm)
    hs = x_sample.reshape(ns, dm)

    c_p, krt_p, kcat_p, q_p = _mla_pre(hp, row(g_mix[0]), wd_ext, row(g_mla_q[0]), row(g_mla_kv[0]), wuq_ext,
                                       wuk_t, cos_p, sin_p, nb=nb, tm=tm_p, dims=mla_dims)
    hp = _mla_attn_prompt(q_p, kcat_p, hp, wuv, wo_mla, nb=nb, t=t, tq=tq, kv_lora=kv_lora)
    hp = _ffn(hp, row(g_ffn[0]), wup[0], wdn[0], row(g_final), tm=tm_p, final_norm=False)

    c_s, krt_s, _, q_s = _mla_pre(hs, row(g_mix[0]), wd_ext, row(g_mla_q[0]), row(g_mla_kv[0]), wuq_ext,
                                  wuk_t, cos_s, sin_s, nb=1, tm=tm_s, dims=mla_dims)
    kw = kv_lora + LANES
    q_dec = q_s.reshape(heads, db, n_new, kw).transpose(1, 0, 2, 3).reshape(db, heads * n_new, kw)
    kr_s = krt_s[0].T
    knew = jnp.concatenate([c_s, kr_s, jnp.zeros((ns, LANES - rope), F32)], axis=-1).reshape(db, n_new, kw)
    knew = jnp.pad(knew, ((0, 0), (0, 8 - n_new), (0, 0)))
    o_lat = _mla_decode(page_table, q_dec, knew, cache_mla_ckv[0], jnp.swapaxes(cache_mla_krope[0], 1, 2),
                        pages=pages, n_new=n_new)
    o_lat = o_lat.reshape(db, heads, n_new, kv_lora).transpose(0, 2, 1, 3).reshape(ns, heads * kv_lora)
    hs = _mla_post(o_lat, hs, wuv, wo_mla, tm=tm_s)
    hs = _ffn(hs, row(g_ffn[0]), wup[0], wdn[0], row(g_final), tm=tm_s, final_norm=False)

    qd_p, k_p, v_p, kb_p, vb_p = _diff_pre(hp, row(g_mix[1]), wqkv, tm=tm_p, heads=dheads, hd2=hd2, nk=nk)
    hp = _diff_attn_prompt(qd_p, kb_p, vb_p, d0, d1, hp, lamp, gsub, wo_diff, nb=nb, t=t, tq=tq,
                           kv_heads=kv_heads, lambda_init=lambda_init)
    y_p = _ffn(hp, row(g_ffn[1]), wup[1], wdn[1], row(g_final), tm=tm_p, final_norm=True)

    qd_s, k_s, v_s, _, _ = _diff_pre(hs, row(g_mix[1]), wqkv, tm=tm_s, heads=dheads, hd2=hd2, nk=nk)
    q6 = qd_s.reshape(kv_heads, group, db, n_new, 2, hd2 // 2).transpose(2, 0, 4, 1, 3, 5)
    place = jnp.eye(2 * kv_heads, dtype=BF16).reshape(kv_heads, 2, 2 * kv_heads)
    q_dec = (q6[..., None, :] * place[None, :, :, None, None, :, None]).reshape(db, 2 * dheads * n_new, nk)
    pad_new = lambda a: jnp.pad(a.reshape(db, n_new, nk), ((0, 0), (0, 8 - n_new), (0, 0)))
    bias_s = _bias_by_distance(rel_bias, page + n_new, page + 1, past + n_new)
    b_last = jnp.stack([bias_s[:, tk + 1:tk + 1 + page][:, ::-1] for tk in range(n_new)], axis=1)
    b_new = jnp.stack([jnp.pad(bias_s[:, :tk + 1][:, ::-1], ((0, 0), (0, LANES - tk - 1))) for tk in range(n_new)],
                      axis=1)

    def per_row(b):
        b = jnp.broadcast_to(b.reshape(kv_heads, 1, group, n_new, -1), (kv_heads, 2, group, n_new, b.shape[-1]))
        return b.reshape(2 * dheads * n_new, -1)

    b_last, b_new = per_row(b_last), per_row(b_new)
    o_d = _diff_decode(page_table, q_dec, pad_new(k_s), pad_new(v_s), b_last, b_new, lamp, gsub,
                       cache_diff_k[0].reshape(-1, page * kv_heads, hd2),
                       cache_diff_v[0].reshape(-1, page * kv_heads, hd2),
                       pages=pages, kv_heads=kv_heads, group=group, n_new=n_new, lambda_init=lambda_init)
    o_d = o_d.reshape(db, kv_heads, group, n_new, hd2).transpose(0, 3, 1, 2, 4).reshape(ns, dheads * hd2)
    hs = _proj_resid(o_d, hs, wo_diff, tm=tm_s)
    y_s = _ffn(hs, row(g_ffn[1]), wup[1], wdn[1], row(g_final), tm=tm_s, final_norm=True)

    return (y_p.reshape(nb, t, dm), y_s.reshape(db, n_new, dm),
            c_p.reshape(1, nb, t, kv_lora), jnp.swapaxes(krt_p, 1, 2)[None],
            k_p.reshape(1, nb, t, kv_heads, hd2), v_p.reshape(1, nb, t, kv_heads, hd2),
            c_s.reshape(1, db, n_new, kv_lora), kr_s.reshape(1, db, n_new, rope),
            k_s.reshape(1, db, n_new, kv_heads, hd2), v_s.reshape(1, db, n_new, kv_heads, hd2))
```

```python
import functools
import math

import numpy as np
import jax
import jax.numpy as jnp
from jax import lax
from jax.experimental import pallas as pl
from jax.experimental.pallas import tpu as pltpu

F32 = jnp.float32
BF16 = jnp.bfloat16

ROPE_THETA = 10000.0
REL_BUCKETS = 32
REL_MAX_DIST = 128
NORM_EPS = 1e-6
SUBLN_EPS = 1e-5
NEG = -1e30
LOG2E = math.log2(math.e)

LANES = 128
ROW_TILE = 512
ATTN_TILE = 256
PAGES_PER_STEP = 32
RING_SLOTS = 3
VMEM_LIMIT = 56 * 1024 * 1024


def _cparams(*sem):
    return pltpu.CompilerParams(dimension_semantics=sem, vmem_limit_bytes=VMEM_LIMIT)


def _rms(x, g, eps):
    return x * lax.rsqrt(jnp.mean(x * x, axis=-1, keepdims=True) + eps) * g


def _dot(a, b):
    return jnp.dot(a, b, preferred_element_type=F32)


def _dot_nt(a, b):
    return lax.dot_general(a, b, (((1,), (1,)), ((), ())), preferred_element_type=F32)


def _lane_block_sum(p):
    out = p[:, :LANES]
    for j in range(1, p.shape[1] // LANES):
        out = out + p[:, j * LANES:(j + 1) * LANES]
    return out


def _softmax_step(s, v, m_sc, l_sc, acc_sc):
    m_prev = m_sc[...]
    m_new = jnp.maximum(m_prev, jnp.max(s, axis=-1, keepdims=True))
    alpha = jnp.exp2(m_prev - m_new)
    p = jnp.exp2(s - jnp.tile(m_new, (1, s.shape[1] // LANES)))
    l_sc[...] = alpha * l_sc[...] + _lane_block_sum(p)
    acc_sc[...] = jnp.tile(alpha, (1, acc_sc.shape[1] // LANES)) * acc_sc[...] + _dot(p.astype(BF16), v)
    m_sc[...] = m_new


def _softmax_init(m_sc, l_sc, acc_sc):
    m_sc[...] = jnp.full(m_sc.shape, NEG, F32)
    l_sc[...] = jnp.zeros(l_sc.shape, F32)
    acc_sc[...] = jnp.zeros(acc_sc.shape, F32)


def _mla_pre_body(h_ref, g_ref, wd_ref, gq_ref, gkv_ref, wuq_ref, wuk_ref, cos_ref, sin_ref,
                  c_ref, kr_ref, kcat_ref, q_ref, *, scale, heads, q_lora, kv_lora, nope, rope):
    tm = h_ref.shape[0]
    a = _rms(h_ref[...], g_ref[...], NORM_EPS).astype(BF16)
    d = _dot(a, wd_ref[...])
    cq = _rms(d[:, :q_lora], gq_ref[...], NORM_EPS)
    c = _rms(d[:, q_lora:q_lora + kv_lora], gkv_ref[...], NORM_EPS)
    cosq = cos_ref[...]
    sinq = sin_ref[...]
    r0 = q_lora + kv_lora
    kr = d[:, r0:r0 + rope] * cosq[:, :rope] + d[:, r0 + rope:r0 + 2 * rope] * sinq[:, :rope]
    c_ref[...] = c
    pad = jnp.zeros((tm, LANES - rope), F32)
    kr_ref[...] = jnp.concatenate([kr, pad], axis=-1).T[:rope]
    kcat_ref[...] = jnp.concatenate([c, kr, pad], axis=-1).astype(BF16)
    q = _dot(cq.astype(BF16), wuq_ref[...])
    hn = heads * nope
    hr = heads * rope
    qr = (q[:, hn:hn + hr] * cosq + q[:, hn + hr:hn + 2 * hr] * sinq) * scale
    for h in range(heads):
        ql = _dot(q[:, h * nope:(h + 1) * nope].astype(BF16), wuk_ref[h]) * scale
        q_ref[h] = jnp.concatenate([ql, qr[:, h * rope:(h + 1) * rope], pad], axis=-1).astype(BF16)


def _mla_pre(h, g, wd, gq, gkv, wuq, wuk, cosq, sinq, *, nb, tm, dims):
    n, dm = h.shape
    t = cosq.shape[0]
    nt = t // tm
    heads, q_lora, kv_lora, nope, rope = dims
    kw = kv_lora + LANES
    row = lambda ti, b: (b * nt + ti, 0)
    const2 = lambda ti, b: (0, 0)
    body = functools.partial(_mla_pre_body, scale=LOG2E * (nope + rope) ** -0.5, heads=heads, q_lora=q_lora,
                             kv_lora=kv_lora, nope=nope, rope=rope)
    return pl.pallas_call(
        body,
        grid=(nt, nb),
        in_specs=[
            pl.BlockSpec((tm, dm), row),
            pl.BlockSpec(g.shape, const2),
            pl.BlockSpec(wd.shape, const2),
            pl.BlockSpec(gq.shape, const2),
            pl.BlockSpec(gkv.shape, const2),
            pl.BlockSpec(wuq.shape, const2),
            pl.BlockSpec(wuk.shape, lambda ti, b: (0, 0, 0)),
            pl.BlockSpec((tm, cosq.shape[1]), lambda ti, b: (ti, 0)),
            pl.BlockSpec((tm, sinq.shape[1]), lambda ti, b: (ti, 0)),
        ],
        out_specs=[
            pl.BlockSpec((tm, kv_lora), row),
            pl.BlockSpec((None, rope, tm), lambda ti, b: (b, 0, ti)),
            pl.BlockSpec((tm, kw), row),
            pl.BlockSpec((heads, tm, kw), lambda ti, b: (0, b * nt + ti, 0)),
        ],
        out_shape=[
            jax.ShapeDtypeStruct((n, kv_lora), F32),
            jax.ShapeDtypeStruct((nb, rope, t), F32),
            jax.ShapeDtypeStruct((n, kw), BF16),
            jax.ShapeDtypeStruct((heads, n, kw), BF16),
        ],
        compiler_params=_cparams("arbitrary", "arbitrary"),
        name="mla_pre",
    )(h, g, wd, gq, gkv, wuq, wuk, cosq, sinq)


def _mla_out(o_heads, wuv_ref, wo_ref, resid):
    ov = [_dot(o.astype(BF16), wuv_ref[h]) for h, o in enumerate(o_heads)]
    o = jnp.concatenate(ov, axis=-1).astype(BF16)
    return resid + _dot(o, wo_ref[...])


def _mla_attn_body(q_ref, k_ref, h_ref, wuv_ref, wo_ref, o_ref, m_sc, l_sc, acc_sc, *, heads, kv_lora, tq):
    i = pl.program_id(1)
    q = q_ref[...].reshape(heads * tq, q_ref.shape[2])
    _softmax_init(m_sc, l_sc, acc_sc)

    def keys(j):
        return k_ref[pl.ds(pl.multiple_of(j * tq, tq), tq), :]

    def full_step(j, s):
        s_next = _dot_nt(q, keys(j + 1))
        _softmax_step(s, keys(j)[:, :kv_lora], m_sc, l_sc, acc_sc)
        return s_next

    s = lax.fori_loop(0, i, full_step, _dot_nt(q, keys(0)))
    row = lax.broadcasted_iota(jnp.int32, (tq, tq), 0)
    col = lax.broadcasted_iota(jnp.int32, (tq, tq), 1)
    s = jnp.where((col <= row)[None], s.reshape(heads, tq, tq), NEG).reshape(heads * tq, tq)
    _softmax_step(s, keys(i)[:, :kv_lora], m_sc, l_sc, acc_sc)

    o = acc_sc[...] / jnp.sum(l_sc[...], axis=-1, keepdims=True)
    o_heads = [o[h * tq:(h + 1) * tq] for h in range(heads)]
    o_ref[...] = _mla_out(o_heads, wuv_ref, wo_ref, h_ref[...])


def _mla_attn_prompt(q, kcat, h, wuv, wo, *, nb, t, tq, kv_lora):
    heads, n, kw = q.shape
    dm = h.shape[1]
    nq = t // tq
    body = functools.partial(_mla_attn_body, heads=heads, kv_lora=kv_lora, tq=tq)
    return pl.pallas_call(
        body,
        grid=(nb, nq),
        in_specs=[
            pl.BlockSpec((heads, tq, kw), lambda b, i: (0, b * nq + i, 0)),
            pl.BlockSpec((t, kw), lambda b, i: (b, 0)),
            pl.BlockSpec((tq, dm), lambda b, i: (b * nq + i, 0)),
            pl.BlockSpec(wuv.shape, lambda b, i: (0, 0, 0)),
            pl.BlockSpec(wo.shape, lambda b, i: (0, 0)),
        ],
        out_specs=pl.BlockSpec((tq, dm), lambda b, i: (b * nq + i, 0)),
        out_shape=jax.ShapeDtypeStruct((n, dm), F32),
        scratch_shapes=[
            pltpu.VMEM((heads * tq, LANES), F32),
            pltpu.VMEM((heads * tq, LANES), F32),
            pltpu.VMEM((heads * tq, kv_lora), F32),
        ],
        compiler_params=_cparams("arbitrary", "arbitrary"),
        name="mla_attn_prompt",
    )(q, kcat, h, wuv, wo)


def _mla_post_body(o_ref, h_ref, wuv_ref, wo_ref, out_ref, *, heads, kv_lora):
    o = o_ref[...]
    o_heads = [o[:, h * kv_lora:(h + 1) * kv_lora] for h in range(heads)]
    out_ref[...] = _mla_out(o_heads, wuv_ref, wo_ref, h_ref[...])


def _mla_post(o_lat, h, wuv, wo, *, tm):
    n, dm = h.shape
    heads, kv_lora, _ = wuv.shape
    body = functools.partial(_mla_post_body, heads=heads, kv_lora=kv_lora)
    return pl.pallas_call(
        body,
        grid=(n // tm,),
        in_specs=[
            pl.BlockSpec((tm, o_lat.shape[1]), lambda i: (i, 0)),
            pl.BlockSpec((tm, dm), lambda i: (i, 0)),
            pl.BlockSpec(wuv.shape, lambda i: (0, 0, 0)),
            pl.BlockSpec(wo.shape, lambda i: (0, 0)),
        ],
        out_specs=pl.BlockSpec((tm, dm), lambda i: (i, 0)),
        out_shape=jax.ShapeDtypeStruct((n, dm), F32),
        compiler_params=_cparams("arbitrary"),
        name="mla_post",
    )(o_lat, h, wuv, wo)


def _ffn_body(h_ref, g_ref, wup_ref, wdn_ref, gf_ref, o_ref, *, chunk, final_norm):
    x = h_ref[...]
    a = _rms(x, g_ref[...], NORM_EPS).astype(BF16)
    acc = x
    for j in range(wup_ref.shape[1] // chunk):
        u = jnp.maximum(_dot(a, wup_ref[:, j * chunk:(j + 1) * chunk]), 0.0)
        acc = acc + _dot((u * u).astype(BF16), wdn_ref[j * chunk:(j + 1) * chunk, :])
    if final_norm:
        acc = _rms(acc, gf_ref[...], NORM_EPS)
    o_ref[...] = acc


def _ffn(h, g, wup, wdn, gf, *, tm, final_norm):
    n, dm = h.shape
    body = functools.partial(_ffn_body, chunk=min(1024, wup.shape[1]), final_norm=final_norm)
    const = lambda i: (0, 0)
    return pl.pallas_call(
        body,
        grid=(n // tm,),
        in_specs=[
            pl.BlockSpec((tm, dm), lambda i: (i, 0)),
            pl.BlockSpec(g.shape, const),
            pl.BlockSpec(wup.shape, const, pipeline_mode=pl.Buffered(1)),
            pl.BlockSpec(wdn.shape, const, pipeline_mode=pl.Buffered(1)),
            pl.BlockSpec(gf.shape, const),
        ],
        out_specs=pl.BlockSpec((tm, dm), lambda i: (i, 0)),
        out_shape=jax.ShapeDtypeStruct((n, dm), F32),
        compiler_params=_cparams("arbitrary"),
        name="ffn_final" if final_norm else "ffn",
    )(h, g, wup, wdn, gf)


def _ffn_diff_pre_body(h_ref, g_ref, wup_ref, wdn_ref, g2_ref, w_ref, o_ref, q_ref, k_ref, v_ref, kb_ref, vb_ref, *,
                       chunk, scale, heads, hd2, nk):
    x = h_ref[...]
    a = _rms(x, g_ref[...], NORM_EPS).astype(BF16)
    acc = x
    for j in range(wup_ref.shape[1] // chunk):
        u = jnp.maximum(_dot(a, wup_ref[:, j * chunk:(j + 1) * chunk]), 0.0)
        acc = acc + _dot((u * u).astype(BF16), wdn_ref[j * chunk:(j + 1) * chunk, :])
    o_ref[...] = acc
    a = _rms(acc, g2_ref[...], NORM_EPS).astype(BF16)
    proj = _dot(a, w_ref[...])
    nq = heads * hd2
    for h in range(heads):
        q_ref[h] = (proj[:, h * hd2:(h + 1) * hd2] * scale).astype(BF16)
    k = proj[:, nq:nq + nk]
    v = proj[:, nq + nk:nq + 2 * nk]
    kv_heads = nk // hd2
    tm = h_ref.shape[0]
    for n in range(kv_heads):
        k_ref[pl.ds(n, tm, stride=kv_heads), :] = k[:, n * hd2:(n + 1) * hd2]
        v_ref[pl.ds(n, tm, stride=kv_heads), :] = v[:, n * hd2:(n + 1) * hd2]
    kb_ref[...] = k.astype(BF16)
    vb_ref[...] = v.astype(BF16)


def _ffn_diff_pre(h, g, wup, wdn, g2, w, *, tm, heads, hd2, nk):
    n, dm = h.shape
    body = functools.partial(_ffn_diff_pre_body, chunk=min(1024, wup.shape[1]), scale=LOG2E * (hd2 // 2) ** -0.5,
                             heads=heads, hd2=hd2, nk=nk)
    row = lambda i: (i, 0)
    const = lambda i: (0, 0)
    return pl.pallas_call(
        body,
        grid=(n // tm,),
        in_specs=[
            pl.BlockSpec((tm, dm), row),
            pl.BlockSpec(g.shape, const),
            pl.BlockSpec(wup.shape, const, pipeline_mode=pl.Buffered(1)),
            pl.BlockSpec(wdn.shape, const, pipeline_mode=pl.Buffered(1)),
            pl.BlockSpec(g2.shape, const),
            pl.BlockSpec(w.shape, const, pipeline_mode=pl.Buffered(1)),
        ],
        out_specs=[
            pl.BlockSpec((tm, dm), row),
            pl.BlockSpec((heads, tm, hd2), lambda i: (0, i, 0)),
            pl.BlockSpec((tm * nk // hd2, hd2), row),
            pl.BlockSpec((tm * nk // hd2, hd2), row),
            pl.BlockSpec((tm, nk), row),
            pl.BlockSpec((tm, nk), row),
        ],
        out_shape=[
            jax.ShapeDtypeStruct((n, dm), F32),
            jax.ShapeDtypeStruct((heads, n, hd2), BF16),
            jax.ShapeDtypeStruct((n * nk // hd2, hd2), F32),
            jax.ShapeDtypeStruct((n * nk // hd2, hd2), F32),
            jax.ShapeDtypeStruct((n, nk), BF16),
            jax.ShapeDtypeStruct((n, nk), BF16),
        ],
        compiler_params=_cparams("arbitrary"),
        name="ffn_diff_pre",
    )(h, g, wup, wdn, g2, w)


def _lambda_value(lam_ref, lambda_init):
    lp = lam_ref[...]
    e1 = jnp.exp(jnp.sum(lp[0:1] * lp[1:2], axis=-1, keepdims=True))
    e2 = jnp.exp(jnp.sum(lp[2:3] * lp[3:4], axis=-1, keepdims=True))
    return e1 - e2 + lambda_init


def _diff_attn_body(q_ref, k_ref, v_ref, d0_ref, d1_ref, h_ref, lam_ref, gsub_ref, wo_ref, o_ref,
                    m_sc, l_sc, acc_sc, *, kv_heads, group, hd2, tq, lambda_init):
    i = pl.program_id(1)
    lam = _lambda_value(lam_ref, lambda_init)
    rows = group * tq
    first_half = lax.broadcasted_iota(jnp.int32, (rows, hd2), 1) < hd2 // 2
    row = lax.broadcasted_iota(jnp.int32, (tq, tq), 0)
    col = lax.broadcasted_iota(jnp.int32, (tq, tq), 1)
    causal = (col <= row)[None, None]
    qzs = []
    for n in range(kv_heads):
        qn = q_ref[n * group:(n + 1) * group].reshape(rows, hd2)
        zero = jnp.zeros_like(qn)
        qzs.append(jnp.concatenate([jnp.where(first_half, qn, zero), jnp.where(first_half, zero, qn)], axis=0))
    _softmax_init(m_sc, l_sc, acc_sc)

    def step(j, bias_ref, masked):
        start = pl.multiple_of(j * tq, tq)
        for n in range(kv_heads):
            lanes = slice(n * hd2, (n + 1) * hd2)
            s = _dot_nt(qzs[n], k_ref[pl.ds(start, tq), lanes])
            if bias_ref is not None:
                s = s.reshape(2, group, tq, tq) + bias_ref[n * group:(n + 1) * group][None]
                if masked:
                    s = jnp.where(causal, s, NEG)
                s = s.reshape(2 * rows, tq)
            _softmax_step(s, v_ref[pl.ds(start, tq), lanes], m_sc.at[n], l_sc.at[n], acc_sc.at[n])

    def far_step(j, carry):
        step(j, None, False)
        return carry

    lax.fori_loop(0, jnp.maximum(i - 1, 0), far_step, 0)

    @pl.when(i >= 1)
    def _():
        step(i - 1, d1_ref, False)

    step(i, d0_ref, True)

    outs = []
    for n in range(kv_heads):
        o = acc_sc[n] / jnp.sum(l_sc[n], axis=-1, keepdims=True)
        o = o[:rows] - lam * o[rows:]
        o = _rms(o, gsub_ref[...], SUBLN_EPS) * (1.0 - lambda_init)
        outs.extend(o[g * tq:(g + 1) * tq] for g in range(group))
    attn = jnp.concatenate(outs, axis=-1).astype(BF16)
    o_ref[...] = h_ref[...] + _dot(attn, wo_ref[...])


def _diff_attn_prompt(q, kb, vb, d0, d1, h, lamp, gsub, wo, *, nb, t, tq, kv_heads, lambda_init):
    heads, n, hd2 = q.shape
    dm = h.shape[1]
    nq = t // tq
    group = heads // kv_heads
    nk = kb.shape[1]
    body = functools.partial(_diff_attn_body, kv_heads=kv_heads, group=group, hd2=hd2, tq=tq,
                             lambda_init=lambda_init)
    c2 = lambda b, i: (0, 0)
    c3 = lambda b, i: (0, 0, 0)
    return pl.pallas_call(
        body,
        grid=(nb, nq),
        in_specs=[
            pl.BlockSpec((heads, tq, hd2), lambda b, i: (0, b * nq + i, 0)),
            pl.BlockSpec((t, nk), lambda b, i: (b, 0)),
            pl.BlockSpec((t, nk), lambda b, i: (b, 0)),
            pl.BlockSpec(d0.shape, c3),
            pl.BlockSpec(d1.shape, c3),
            pl.BlockSpec((tq, dm), lambda b, i: (b * nq + i, 0)),
            pl.BlockSpec(lamp.shape, c2),
            pl.BlockSpec(gsub.shape, c2),
            pl.BlockSpec(wo.shape, c2),
        ],
        out_specs=pl.BlockSpec((tq, dm), lambda b, i: (b * nq + i, 0)),
        out_shape=jax.ShapeDtypeStruct((n, dm), F32),
        scratch_shapes=[
            pltpu.VMEM((kv_heads, 2 * group * tq, LANES), F32),
            pltpu.VMEM((kv_heads, 2 * group * tq, LANES), F32),
            pltpu.VMEM((kv_heads, 2 * group * tq, hd2), F32),
        ],
        compiler_params=_cparams("arbitrary", "arbitrary"),
        name="diff_attn_prompt",
    )(q, kb, vb, d0, d1, h, lamp, gsub, wo)


def _proj_resid_body(x_ref, h_ref, w_ref, o_ref):
    o_ref[...] = h_ref[...] + _dot(x_ref[...].astype(BF16), w_ref[...])


def _proj_resid(x, h, w, *, tm):
    n, dm = h.shape
    return pl.pallas_call(
        _proj_resid_body,
        grid=(n // tm,),
        in_specs=[
            pl.BlockSpec((tm, x.shape[1]), lambda i: (i, 0)),
            pl.BlockSpec((tm, dm), lambda i: (i, 0)),
            pl.BlockSpec(w.shape, lambda i: (0, 0)),
        ],
        out_specs=pl.BlockSpec((tm, dm), lambda i: (i, 0)),
        out_shape=jax.ShapeDtypeStruct((n, dm), F32),
        compiler_params=_cparams("arbitrary"),
        name="proj_resid",
    )(x, h, w)


def _new_token_update(qf, k_new, v_new, bias_new, valid, m_sc, l_sc, acc_sc):
    n_new = valid.shape[1]
    lane0 = lax.broadcasted_iota(jnp.int32, (qf.shape[0], LANES), 1) == 0
    s_cols = []
    for t in range(n_new):
        s_t = jnp.sum(qf * k_new[t:t + 1, :], axis=-1, keepdims=True)
        if bias_new is not None:
            s_t = s_t + bias_new[:, t:t + 1]
        s_cols.append(jnp.where(valid[:, t:t + 1], s_t, NEG))
    m_prev = m_sc[...]
    m_new = m_prev
    for s_t in s_cols:
        m_new = jnp.maximum(m_new, s_t)
    alpha = jnp.exp2(m_prev - m_new)
    l = alpha * l_sc[...]
    acc = jnp.tile(alpha, (1, acc_sc.shape[1] // LANES)) * acc_sc[...]
    for t, s_t in enumerate(s_cols):
        p_t = jnp.exp2(s_t - m_new)
        l = l + jnp.where(lane0, p_t, 0.0)
        acc = acc + jnp.tile(p_t, (1, acc_sc.shape[1] // LANES)) * v_new[t:t + 1, :]
    m_sc[...] = m_new
    l_sc[...] = l
    acc_sc[...] = acc


def _page_ring(pt_ref, srcs, bufs, sems, *, pages):
    nchunks = pl.num_programs(1)
    step = pl.program_id(0) * nchunks + pl.program_id(1)
    total = pl.num_programs(0) * nchunks
    ahead = RING_SLOTS - 1
    slot = lax.rem(step, RING_SLOTS)

    def copies(step_idx, slot_idx):
        out = []
        for p in range(pages):
            pg = pt_ref[step_idx * pages + p]
            for src, buf, sem in zip(srcs, bufs, sems):
                out.append(pltpu.make_async_copy(src.at[pg], buf.at[slot_idx, p], sem.at[slot_idx]))
        return out

    def start_all(step_idx, slot_idx):
        for n, cp in enumerate(copies(step_idx, slot_idx)):
            cp.start(priority=(n // len(srcs)) % 2)

    for first in range(ahead):
        @pl.when((step == 0) & (first < total))
        def _():
            start_all(first, first)

    @pl.when(step + ahead < total)
    def _():
        start_all(step + ahead, lax.rem(step + ahead, RING_SLOTS))

    for cp in copies(step, slot):
        cp.wait()
    return slot


def _mla_decode_body(pt_ref, q_ref, knew_ref, ckv_hbm, kr_hbm, o_ref, ckv_pg, kr_pg, cbuf, krbuf, m_sc, l_sc, acc_sc,
                     sem_c, sem_r, *, pages, page, kv_lora, rope, n_new):
    c = pl.program_id(1)
    slot = _page_ring(pt_ref, (ckv_hbm, kr_hbm), (ckv_pg, kr_pg), (sem_c, sem_r), pages=pages)

    @pl.when(c == 0)
    def _():
        _softmax_init(m_sc, l_sc, acc_sc)

    krbuf[rope:, :] = jnp.zeros((krbuf.shape[0] - rope, krbuf.shape[1]), BF16)
    for p in range(pages):
        cbuf[p * page:(p + 1) * page, :] = ckv_pg[slot, p].astype(BF16)
        krbuf[:rope, p * page:(p + 1) * page] = kr_pg[slot, p].astype(BF16)
    q = q_ref[...]
    ck = cbuf[...]
    s = _dot_nt(q[:, :kv_lora], ck) + _dot(q[:, kv_lora:], krbuf[...])
    _softmax_step(s, ck, m_sc, l_sc, acc_sc)

    @pl.when(c == pl.num_programs(1) - 1)
    def _():
        r = q.shape[0]
        k_new = knew_ref[...]
        t_row = lax.broadcasted_iota(jnp.int32, (r, n_new), 0) % n_new
        t_col = lax.broadcasted_iota(jnp.int32, (r, n_new), 1)
        _new_token_update(q.astype(F32), k_new, k_new[:, :kv_lora], None, t_col <= t_row, m_sc, l_sc, acc_sc)
        o_ref[...] = acc_sc[...] / jnp.sum(l_sc[...], axis=-1, keepdims=True)


def _mla_decode(page_table, q, knew, cache_ckv, cache_kr, *, pages, n_new):
    db, r, kw = q.shape
    n_pages = page_table.shape[1]
    page, kv_lora = cache_ckv.shape[1:]
    rope = cache_kr.shape[1]
    nchunks = n_pages // pages
    pt = page_table.reshape(-1)

    body = functools.partial(_mla_decode_body, pages=pages, page=page, kv_lora=kv_lora, rope=rope, n_new=n_new)
    grid_spec = pltpu.PrefetchScalarGridSpec(
        num_scalar_prefetch=1,
        grid=(db, nchunks),
        in_specs=[
            pl.BlockSpec((None, r, kw), lambda b, c, pt_ref: (b, 0, 0)),
            pl.BlockSpec((None,) + knew.shape[1:], lambda b, c, pt_ref: (b, 0, 0)),
            pl.BlockSpec(memory_space=pl.ANY),
            pl.BlockSpec(memory_space=pl.ANY),
        ],
        out_specs=pl.BlockSpec((None, r, kv_lora), lambda b, c, pt_ref: (b, 0, 0)),
        scratch_shapes=[
            pltpu.VMEM((RING_SLOTS, pages, page, kv_lora), F32),
            pltpu.VMEM((RING_SLOTS, pages, rope, page), F32),
            pltpu.VMEM((pages * page, kv_lora), BF16),
            pltpu.VMEM((kw - kv_lora, pages * page), BF16),
            pltpu.VMEM((r, LANES), F32),
            pltpu.VMEM((r, LANES), F32),
            pltpu.VMEM((r, kv_lora), F32),
            pltpu.SemaphoreType.DMA((RING_SLOTS,)),
            pltpu.SemaphoreType.DMA((RING_SLOTS,)),
        ],
    )
    return pl.pallas_call(
        body,
        grid_spec=grid_spec,
        out_shape=jax.ShapeDtypeStruct((db, r, kv_lora), F32),
        compiler_params=_cparams("arbitrary", "arbitrary"),
        name="mla_decode",
    )(pt, q, knew, cache_ckv, cache_kr)


def _diff_decode_body(pt_ref, q_ref, knew_ref, vnew_ref, blast_ref, bnew_ref, lam_ref, gsub_ref, k_hbm, v_hbm,
                      o_ref, k_pg, v_pg, kbuf, vbuf, s_sc, m_sc, l_sc, acc_sc, sem_k, sem_v, *,
                      pages, page, kv_heads, group, hd2, n_new, lambda_init):
    c = pl.program_id(1)
    last = pl.num_programs(1) - 1
    slot = _page_ring(pt_ref, (k_hbm, v_hbm), (k_pg, v_pg), (sem_k, sem_v), pages=pages)

    @pl.when(c == 0)
    def _():
        _softmax_init(m_sc, l_sc, acc_sc)

    for p in range(pages):
        rows = slice(p * page, (p + 1) * page)
        for n in range(kv_heads):
            lanes = slice(n * hd2, (n + 1) * hd2)
            kbuf[rows, lanes] = k_pg[slot, p, pl.ds(n, page, stride=kv_heads), :].astype(BF16)
            vbuf[rows, lanes] = v_pg[slot, p, pl.ds(n, page, stride=kv_heads), :].astype(BF16)
    q = q_ref[...]
    s_sc[...] = _dot_nt(q, kbuf[...])

    @pl.when(c == last)
    def _():
        cols = slice((pages - 1) * page, pages * page)
        s_sc[:, cols] = s_sc[:, cols] + blast_ref[...]

    _softmax_step(s_sc[...], vbuf[...], m_sc, l_sc, acc_sc)

    @pl.when(c == last)
    def _():
        r = q.shape[0]
        t_row = lax.broadcasted_iota(jnp.int32, (r, n_new), 0) % n_new
        t_col = lax.broadcasted_iota(jnp.int32, (r, n_new), 1)
        _new_token_update(q.astype(F32), knew_ref[...], vnew_ref[...], bnew_ref[...], t_col <= t_row,
                          m_sc, l_sc, acc_sc)
        o = acc_sc[...] / jnp.sum(l_sc[...], axis=-1, keepdims=True)
        lam = _lambda_value(lam_ref, lambda_init)
        gr = group * n_new
        outs = []
        for n in range(kv_heads):
            o1 = o[(2 * n) * gr:(2 * n + 1) * gr, n * hd2:(n + 1) * hd2]
            o2 = o[(2 * n + 1) * gr:(2 * n + 2) * gr, n * hd2:(n + 1) * hd2]
            outs.append(o1 - lam * o2)
        on = jnp.concatenate(outs, axis=0)
        o_ref[...] = _rms(on, gsub_ref[...], SUBLN_EPS) * (1.0 - lambda_init)


def _diff_decode(page_table, q, knew, vnew, blast, bnew, lamp, gsub, cache_k, cache_v, *, pages, kv_heads,
                 group, n_new, lambda_init):
    db, r, nk = q.shape
    n_pages = page_table.shape[1]
    hd2 = nk // kv_heads
    page = cache_k.shape[1] // kv_heads
    nchunks = n_pages // pages
    pt = page_table.reshape(-1)

    per_b = lambda b, c, pt_ref: (b, 0, 0)
    const = lambda b, c, pt_ref: (0, 0)
    body = functools.partial(_diff_decode_body, pages=pages, page=page, kv_heads=kv_heads, group=group,
                             hd2=hd2, n_new=n_new, lambda_init=lambda_init)
    r_out = kv_heads * group * n_new
    grid_spec = pltpu.PrefetchScalarGridSpec(
        num_scalar_prefetch=1,
        grid=(db, nchunks),
        in_specs=[
            pl.BlockSpec((None, r, nk), per_b),
            pl.BlockSpec((None,) + knew.shape[1:], per_b),
            pl.BlockSpec((None,) + vnew.shape[1:], per_b),
            pl.BlockSpec(blast.shape, const),
            pl.BlockSpec(bnew.shape, const),
            pl.BlockSpec(lamp.shape, const),
            pl.BlockSpec(gsub.shape, const),
            pl.BlockSpec(memory_space=pl.ANY),
            pl.BlockSpec(memory_space=pl.ANY),
        ],
        out_specs=pl.BlockSpec((None, r_out, hd2), per_b),
        scratch_shapes=[
            pltpu.VMEM((RING_SLOTS, pages, page * kv_heads, hd2), F32),
            pltpu.VMEM((RING_SLOTS, pages, page * kv_heads, hd2), F32),
            pltpu.VMEM((pages * page, nk), BF16),
            pltpu.VMEM((pages * page, nk), BF16),
            pltpu.VMEM((r, pages * page), F32),
            pltpu.VMEM((r, LANES), F32),
            pltpu.VMEM((r, LANES), F32),
            pltpu.VMEM((r, nk), F32),
            pltpu.SemaphoreType.DMA((RING_SLOTS,)),
            pltpu.SemaphoreType.DMA((RING_SLOTS,)),
        ],
    )
    return pl.pallas_call(
        body,
        grid_spec=grid_spec,
        out_shape=jax.ShapeDtypeStruct((db, r_out, hd2), F32),
        compiler_params=_cparams("arbitrary", "arbitrary"),
        name="diff_decode",
    )(pt, q, knew, vnew, blast, bnew, lamp, gsub, cache_k, cache_v)


def _rope_tables(pos, half, heads):
    inv = ROPE_THETA ** (-jnp.arange(half, dtype=F32) / half)
    ang = pos.astype(F32)[:, None] * inv[None, :]
    cos, sin = jnp.cos(ang), jnp.sin(ang)
    cos2 = jnp.concatenate([cos, cos], axis=-1)
    sin2 = jnp.concatenate([-sin, sin], axis=-1)
    return jnp.tile(cos2, (1, heads)), jnp.tile(sin2, (1, heads))


def _bucket_of_distance(n):
    max_exact = REL_BUCKETS // 2
    nf = np.maximum(n, 1).astype(np.float32)
    ratio = np.log(nf / np.float32(max_exact)) / np.float32(math.log(REL_MAX_DIST / max_exact))
    large = max_exact + (ratio * np.float32(REL_BUCKETS - max_exact)).astype(np.int32)
    large = np.minimum(large, REL_BUCKETS - 1)
    return np.where(n < max_exact, n, large)


def _bias_by_distance(rel_bias, n, far_from, max_dist):
    far = _bucket_of_distance(np.arange(far_from, max_dist + 1))
    assert (far == far[0]).all(), "relative-position bias must be constant beyond the near tiles"
    select = np.zeros((n, REL_BUCKETS), np.float32)
    select[np.arange(n), _bucket_of_distance(np.arange(n))] = 1.0
    select[:, int(far[0])] -= 1.0
    picked = jnp.dot(jnp.asarray(select), rel_bias.astype(F32), precision=lax.Precision.HIGHEST)
    return LOG2E * picked.T


def _toeplitz(v, n, offset):
    heads = v.shape[0]
    length = 2 * n
    lo = n - 1 - offset
    u = jnp.pad(v, ((0, 0), (lo, 0)))[:, :length] if lo >= 0 else jnp.pad(v[:, -lo:], ((0, 0), (0, -lo)))
    x = jnp.tile(u, (1, n + 1))[:, :n * (length + 1)].reshape(heads, n, length + 1)[:, :, :n]
    return x[:, :, ::-1]


def kernel(x_prompt, x_sample, cache_mla_ckv, cache_mla_krope, cache_diff_k, cache_diff_v, page_table, g_mix, g_ffn, g_final, w_mla_down, g_mla_q, g_mla_kv, w_mla_uq, w_mla_uk, w_mla_uv, w_mla_o, w_diff_qkv, lam_q1, lam_k1, lam_q2, lam_k2, g_diff_sub, w_diff_o, rel_bias, w_ff_up, w_ff_down):
    nb, t, dm = x_prompt.shape
    db, n_new, _ = x_sample.shape
    n_pages = page_table.shape[1]
    page = cache_mla_ckv.shape[2]
    past = n_pages * page
    kv_lora, heads, nope = w_mla_uk.shape[1:]
    q_lora = g_mla_q.shape[1]
    rope = cache_mla_krope.shape[3]
    kv_heads, hd2 = cache_diff_k.shape[3:]
    nk = kv_heads * hd2
    dheads = rel_bias.shape[1]
    group = dheads // kv_heads
    assert g_mix.shape[0] == 2 and w_mla_down.shape[0] == 1 and w_diff_qkv.shape[0] == 1
    assert heads * rope == 4 * LANES and kv_lora % LANES == 0 and hd2 == LANES and rope * 2 == LANES

    tm_p = min(ROW_TILE, t)
    ns = db * n_new
    tm_s = min(ROW_TILE, ns)
    tq = min(ATTN_TILE, t)
    pages = min(PAGES_PER_STEP, n_pages)
    assert t % tm_p == 0 and ns % tm_s == 0 and t % tq == 0 and n_pages % pages == 0
    lambda_init = 0.8 - 0.6 * math.exp(-0.3 * 1)

    row = lambda v: v.reshape(1, -1).astype(F32)
    bf = lambda w: w.astype(BF16)

    wd = w_mla_down[0]
    r0 = q_lora + kv_lora
    half = rope // 2
    wd_ext = bf(jnp.concatenate([wd, wd[:, r0 + half:r0 + rope], wd[:, r0:r0 + half]], axis=1))
    wuq = w_mla_uq[0].reshape(q_lora, heads, nope + rope)
    wuq_ext = bf(jnp.concatenate([
        wuq[:, :, :nope].reshape(q_lora, heads * nope),
        wuq[:, :, nope:].reshape(q_lora, heads * rope),
        jnp.concatenate([wuq[:, :, nope + half:], wuq[:, :, nope:nope + half]], axis=-1).reshape(q_lora, heads * rope),
    ], axis=1))
    wuk_t = bf(jnp.transpose(w_mla_uk[0], (1, 2, 0)))
    wuv = bf(jnp.transpose(w_mla_uv[0], (1, 0, 2)))
    wo_mla = bf(w_mla_o[0])
    wqkv = bf(w_diff_qkv[0])
    wo_diff = bf(w_diff_o[0])
    wup = bf(w_ff_up)
    wdn = bf(w_ff_down)
    lamp = jnp.stack([lam_q1[0], lam_k1[0], lam_q2[0], lam_k2[0]]).astype(F32)
    gsub = row(g_diff_sub[0])
    mla_dims = (heads, q_lora, kv_lora, nope, rope)

    cos_p, sin_p = _rope_tables(jnp.arange(t, dtype=jnp.int32), half, heads)
    pos_s = past + (jnp.arange(ns, dtype=jnp.int32) % n_new)
    cos_s, sin_s = _rope_tables(pos_s, half, heads)
    bias_p = _bias_by_distance(rel_bias, 2 * tq, tq + 1, t + tq)
    d0 = _toeplitz(bias_p, tq, 0)
    d1 = _toeplitz(bias_p, tq, tq)

    hp = x_prompt.reshape(nb * t, dm)
    hs = x_sample.reshape(ns, dm)

    c_p, krt_p, kcat_p, q_p = _mla_pre(hp, row(g_mix[0]), wd_ext, row(g_mla_q[0]), row(g_mla_kv[0]), wuq_ext,
                                       wuk_t, cos_p, sin_p, nb=nb, tm=tm_p, dims=mla_dims)
    hp = _mla_attn_prompt(q_p, kcat_p, hp, wuv, wo_mla, nb=nb, t=t, tq=tq, kv_lora=kv_lora)
    hp, qd_p, k_p, v_p, kb_p, vb_p = _ffn_diff_pre(hp, row(g_ffn[0]), wup[0], wdn[0], row(g_mix[1]), wqkv, tm=tm_p,
                                                   heads=dheads, hd2=hd2, nk=nk)

    c_s, krt_s, _, q_s = _mla_pre(hs, row(g_mix[0]), wd_ext, row(g_mla_q[0]), row(g_mla_kv[0]), wuq_ext,
                                  wuk_t, cos_s, sin_s, nb=1, tm=tm_s, dims=mla_dims)
    kw = kv_lora + LANES
    q_dec = q_s.reshape(heads, db, n_new, kw).transpose(1, 0, 2, 3).reshape(db, heads * n_new, kw)
    kr_s = krt_s[0].T
    knew = jnp.concatenate([c_s, kr_s, jnp.zeros((ns, LANES - rope), F32)], axis=-1).reshape(db, n_new, kw)
    knew = jnp.pad(knew, ((0, 0), (0, 8 - n_new), (0, 0)))
    o_lat = _mla_decode(page_table, q_dec, knew, cache_mla_ckv[0], jnp.swapaxes(cache_mla_krope[0], 1, 2),
                        pages=pages, n_new=n_new)
    o_lat = o_lat.reshape(db, heads, n_new, kv_lora).transpose(0, 2, 1, 3).reshape(ns, heads * kv_lora)
    hs = _mla_post(o_lat, hs, wuv, wo_mla, tm=tm_s)
    hs, qd_s, k_s, v_s, _, _ = _ffn_diff_pre(hs, row(g_ffn[0]), wup[0], wdn[0], row(g_mix[1]), wqkv, tm=tm_s,
                                             heads=dheads, hd2=hd2, nk=nk)

    hp = _diff_attn_prompt(qd_p, kb_p, vb_p, d0, d1, hp, lamp, gsub, wo_diff, nb=nb, t=t, tq=tq,
                           kv_heads=kv_heads, lambda_init=lambda_init)
    y_p = _ffn(hp, row(g_ffn[1]), wup[1], wdn[1], row(g_final), tm=tm_p, final_norm=True)

    q6 = qd_s.reshape(kv_heads, group, db, n_new, 2, hd2 // 2).transpose(2, 0, 4, 1, 3, 5)
    place = jnp.eye(2 * kv_heads, dtype=BF16).reshape(kv_heads, 2, 2 * kv_heads)
    q_dec = (q6[..., None, :] * place[None, :, :, None, None, :, None]).reshape(db, 2 * dheads * n_new, nk)
    pad_new = lambda a: jnp.pad(a.reshape(db, n_new, nk), ((0, 0), (0, 8 - n_new), (0, 0)))
    bias_s = _bias_by_distance(rel_bias, page + n_new, page + 1, past + n_new)
    b_last = jnp.stack([bias_s[:, tk + 1:tk + 1 + page][:, ::-1] for tk in range(n_new)], axis=1)
    b_new = jnp.stack([jnp.pad(bias_s[:, :tk + 1][:, ::-1], ((0, 0), (0, LANES - tk - 1))) for tk in range(n_new)],
                      axis=1)

    def per_row(b):
        b = jnp.broadcast_to(b.reshape(kv_heads, 1, group, n_new, -1), (kv_heads, 2, group, n_new, b.shape[-1]))
        return b.reshape(2 * dheads * n_new, -1)

    b_last, b_new = per_row(b_last), per_row(b_new)
    o_d = _diff_decode(page_table, q_dec, pad_new(k_s), pad_new(v_s), b_last, b_new, lamp, gsub,
                       cache_diff_k[0].reshape(-1, page * kv_heads, hd2),
                       cache_diff_v[0].reshape(-1, page * kv_heads, hd2),
                       pages=pages, kv_heads=kv_heads, group=group, n_new=n_new, lambda_init=lambda_init)
    o_d = o_d.reshape(db, kv_heads, group, n_new, hd2).transpose(0, 3, 1, 2, 4).reshape(ns, dheads * hd2)
    hs = _proj_resid(o_d, hs, wo_diff, tm=tm_s)
    y_s = _ffn(hs, row(g_ffn[1]), wup[1], wdn[1], row(g_final), tm=tm_s, final_norm=True)

    return (y_p.reshape(nb, t, dm), y_s.reshape(db, n_new, dm),
            c_p.reshape(1, nb, t, kv_lora), jnp.swapaxes(krt_p, 1, 2)[None],
            k_p.reshape(1, nb, t, kv_heads, hd2), v_p.reshape(1, nb, t, kv_heads, hd2),
            c_s.reshape(1, db, n_new, kv_lora), kr_s.reshape(1, db, n_new, rope),
            k_s.reshape(1, db, n_new, kv_heads, hd2), v_s.reshape(1, db, n_new, kv_heads, hd2))
```
